```python
import jax, jax.numpy as jnp
from jax import lax
import numpy as np

D_MODEL = 1024
BATCH = 8
SEQ = 4096
DEPTH = 1

CTX_LEN = 256
GRID_W = 64
EPS = 1e-6
CHUNK = 128
A_GROUPS = 8
A_WIDTH = 1024
A_GROUP_DIM = A_WIDTH // A_GROUPS
N_HEADS = 16
N_KV_HEADS = 4
HEAD_DIM = 64
B_WIDTH = N_HEADS * HEAD_DIM
KV_WIDTH = N_KV_HEADS * HEAD_DIM
WINDOW = 128
BLOCK = 128
ROPE_BASE = 10000.0
N_EXPERTS = 16
CAPACITY_FACTOR = 2
D_EXPERT = 2048
OFF_U = 0
OFF_VA = OFF_U + A_WIDTH
OFF_Q = OFF_VA + A_WIDTH
OFF_K = OFF_Q + B_WIDTH
OFF_VB = OFF_K + KV_WIDTH
OFF_GATE = OFF_VB + KV_WIDTH
IN_COLS = OFF_GATE + 2 * D_MODEL

kernel_name = "hybrid_gmlp_window_gqa_ec_moe_dit"


def rmsnorm(x, g):
    xf = x.astype(jnp.float32)
    y = xf * lax.rsqrt(jnp.mean(xf * xf, axis=-1, keepdims=True) + EPS)
    return (y * g.astype(jnp.float32)).astype(x.dtype)


def layernorm(x, g, b):
    xf = x.astype(jnp.float32)
    mu = jnp.mean(xf, axis=-1, keepdims=True)
    var = jnp.mean(jnp.square(xf - mu), axis=-1, keepdims=True)
    y = (xf - mu) * lax.rsqrt(var + EPS)
    return (y * g.astype(jnp.float32) + b.astype(jnp.float32)).astype(x.dtype)


def modulate(h, shift, scale):
    return h * (1 + scale) + shift


def _axis_angles(pos, dim):
    freqs = ROPE_BASE ** (-jnp.arange(0, dim, 2, dtype=jnp.float32) / dim)
    return pos.astype(jnp.float32)[:, None] * freqs[None, :]


def _rotate(x, ang):
    x1, x2 = jnp.split(x.astype(jnp.float32), 2, axis=-1)
    cos = jnp.cos(ang)[None, :, None, :]
    sin = jnp.sin(ang)[None, :, None, :]
    return jnp.concatenate([x1 * cos - x2 * sin, x1 * sin + x2 * cos], axis=-1)


def rope_2d(x, row, col):
    half = x.shape[-1] // 2
    xr, xc = x[..., :half], x[..., half:]
    out = jnp.concatenate([_rotate(xr, _axis_angles(row, half)),
                           _rotate(xc, _axis_angles(col, half))], axis=-1)
    return out.astype(x.dtype)


def chunk_gmlp(u, v, ln_g, ln_b, w_s, b_s):
    Bn, L, _ = v.shape
    v = layernorm(v, ln_g, ln_b)
    vc = v.reshape(Bn, L // CHUNK, CHUNK, A_GROUPS, A_GROUP_DIM)
    mixed = jnp.einsum('gpq,bnqgd->bnpgd', w_s, vc) + b_s.T[None, None, :, :, None]
    return u * mixed.reshape(Bn, L, A_WIDTH)


def window_gqa(q, k, v, k_ctx, v_ctx, sink):
    Bn, L, H, Dh = q.shape
    nb = L // BLOCK
    rep = H // N_KV_HEADS
    scale = HEAD_DIM ** -0.5
    qb = q.reshape(Bn, nb, BLOCK, N_KV_HEADS, rep, Dh)

    def band(t):
        tb = t.reshape(Bn, nb, BLOCK, N_KV_HEADS, Dh)
        tp = jnp.pad(tb, ((0, 0), (1, 1), (0, 0), (0, 0), (0, 0)))
        return jnp.concatenate([tp[:, :-2], tp[:, 1:-1], tp[:, 2:]], axis=2)

    kb, vb = band(k), band(v)
    a_idx = jnp.arange(BLOCK)[:, None]
    s_idx = jnp.arange(3 * BLOCK)[None, :]
    rel = s_idx - BLOCK - a_idx
    sink_l = sink.astype(jnp.float32).reshape(N_KV_HEADS, rep)[None, :, :, None, None]
    vals_ctx = v_ctx.astype(jnp.float32)

    def one_block(args):
        n, qn, kn, vn = args
        key_pos = (n - 1) * BLOCK + s_idx
        mask = (jnp.abs(rel) <= WINDOW) & (key_pos >= 0) & (key_pos < L)
        s_loc = jnp.einsum('bqkrd,bskd->bkrqs', qn, kn, preferred_element_type=jnp.float32) * scale
        s_loc = jnp.where(mask, s_loc, jnp.float32(-1e30))
        s_ctx = jnp.einsum('bqkrd,bskd->bkrqs', qn, k_ctx, preferred_element_type=jnp.float32) * scale
        logits = jnp.concatenate([s_loc, s_ctx], axis=-1)
        m = jnp.maximum(jnp.max(logits, axis=-1, keepdims=True), sink_l)
        p = jnp.exp(logits - m)
        denom = jnp.sum(p, axis=-1, keepdims=True) + jnp.exp(sink_l - m)
        vals = jnp.concatenate([vn.astype(jnp.float32), vals_ctx], axis=1)
        out = jnp.einsum('bkrqs,bskd->bqkrd', p / denom, vals)
        return out.astype(q.dtype)

    xs = (jnp.arange(nb), jnp.moveaxis(qb, 1, 0), jnp.moveaxis(kb, 1, 0), jnp.moveaxis(vb, 1, 0))
    out = lax.map(one_block, xs)
    return jnp.moveaxis(out, 0, 1).reshape(Bn, L, H * Dh)


def expert_choice_ffn(h, w_router, w_gate, w_up, w_down):
    Bn, N, D = h.shape
    cap = max(1, CAPACITY_FACTOR * N // N_EXPERTS)
    logits = jnp.einsum('bnd,de->bne', h, w_router, preferred_element_type=jnp.float32)
    aff = jax.nn.softmax(logits, axis=-1)
    top_w, top_idx = lax.top_k(jnp.swapaxes(aff, 1, 2), cap)
    xe = jax.vmap(lambda hb, ib: hb[ib])(h, top_idx)
    g = jnp.einsum('becd,edf->becf', xe, w_gate)
    u = jnp.einsum('becd,edf->becf', xe, w_up)
    y = jnp.einsum('becf,efd->becd', jax.nn.silu(g) * u, w_down)
    y = y * top_w[..., None].astype(y.dtype)
    return jax.vmap(lambda ib, yb: jnp.zeros((N, D), yb.dtype).at[ib.reshape(-1)].add(yb.reshape(-1, D)))(top_idx, y)


def setup_inputs(seed: int = 0) -> dict:
    key = jax.random.key(seed)
    ks = jax.random.split(key, 24)
    f32 = jnp.float32
    D = D_MODEL

    def nrm(k, shape, s):
        return jax.random.normal(k, shape, f32) * s

    return {
        "x": nrm(ks[0], (BATCH, SEQ, D), 1.0),
        "c": nrm(ks[1], (BATCH, D), 1.0),
        "ctx": nrm(ks[2], (BATCH, CTX_LEN, D), 1.0),
        "c_ctx": nrm(ks[3], (D,), 1.0),
        "w_ada": nrm(ks[4], (DEPTH, D, 6 * D), 0.3 * D ** -0.5),
        "b_ada": nrm(ks[5], (DEPTH, 6 * D), 0.02),
        "norm1_g": 1.0 + nrm(ks[6], (DEPTH, D), 0.05),
        "w_in": nrm(ks[7], (DEPTH, D, IN_COLS), D ** -0.5),
        "b_merge_gate": nrm(ks[8], (DEPTH, 2 * D), 0.02),
        "gmlp_ln_g": 1.0 + nrm(ks[9], (DEPTH, A_WIDTH), 0.05),
        "gmlp_ln_b": nrm(ks[10], (DEPTH, A_WIDTH), 0.02),
        "w_spatial": nrm(ks[11], (DEPTH, A_GROUPS, CHUNK, CHUNK), CHUNK ** -0.5),
        "b_spatial": nrm(ks[12], (DEPTH, A_GROUPS, CHUNK), 0.02),
        "attn_sink": nrm(ks[13], (DEPTH, N_HEADS), 0.5),
        "w_proj_a": nrm(ks[14], (DEPTH, A_WIDTH, D), A_WIDTH ** -0.5),
        "w_proj_b": nrm(ks[15], (DEPTH, B_WIDTH, D), B_WIDTH ** -0.5),
        "w_out": nrm(ks[16], (DEPTH, D, D), D ** -0.5),
        "norm2_g": 1.0 + nrm(ks[17], (DEPTH, D), 0.05),
        "w_router": nrm(ks[18], (DEPTH, D, N_EXPERTS), D ** -0.5),
        "w_exp_gate": nrm(ks[19], (DEPTH, N_EXPERTS, D, D_EXPERT), D ** -0.5),
        "w_exp_up": nrm(ks[20], (DEPTH, N_EXPERTS, D, D_EXPERT), D ** -0.5),
        "w_exp_down": nrm(ks[21], (DEPTH, N_EXPERTS, D_EXPERT, D), D_EXPERT ** -0.5),
        "final_g": 1.0 + nrm(ks[22], (D,), 0.05),
    }


def reference(x, c, ctx, c_ctx, w_ada, b_ada, norm1_g, w_in, b_merge_gate, gmlp_ln_g, gmlp_ln_b,
              w_spatial, b_spatial, attn_sink, w_proj_a, w_proj_b, w_out, norm2_g, w_router,
              w_exp_gate, w_exp_up, w_exp_down, final_g):
    Bn, L, D = x.shape
    rows = L // GRID_W
    row = jnp.repeat(jnp.arange(rows), GRID_W)
    col = jnp.tile(jnp.arange(GRID_W), rows)

    for l in range(DEPTH):
        mod = jax.nn.silu(c) @ w_ada[l] + b_ada[l]
        sh1, sc1, g1, sh2, sc2, g2 = jnp.split(mod[:, None, :], 6, axis=-1)
        mod_ctx = jax.nn.silu(c_ctx) @ w_ada[l] + b_ada[l]
        csh1, csc1 = mod_ctx[:D], mod_ctx[D:2 * D]

        h = modulate(rmsnorm(x, norm1_g[l]), sh1, sc1)
        proj = h @ w_in[l]
        u_a, v_a, q, k, v, gates = jnp.split(proj, [OFF_VA, OFF_Q, OFF_K, OFF_VB, OFF_GATE], axis=-1)

        hc = modulate(rmsnorm(ctx, norm1_g[l]), csh1, csc1)
        kv_ctx = hc @ w_in[l][:, OFF_K:OFF_GATE]
        k_ctx, v_ctx = jnp.split(kv_ctx.reshape(Bn, -1, 2 * N_KV_HEADS, HEAD_DIM), 2, axis=2)

        y_a = chunk_gmlp(jax.nn.gelu(u_a), jax.nn.gelu(v_a), gmlp_ln_g[l], gmlp_ln_b[l],
                         w_spatial[l], b_spatial[l])

        qh = rope_2d(q.reshape(Bn, L, N_HEADS, HEAD_DIM), row, col)
        kh = rope_2d(k.reshape(Bn, L, N_KV_HEADS, HEAD_DIM), row, col)
        vh = v.reshape(Bn, L, N_KV_HEADS, HEAD_DIM)
        y_b = window_gqa(qh, kh, vh, k_ctx, v_ctx, attn_sink[l])

        g_a, g_b = jnp.split(jax.nn.sigmoid(gates + b_merge_gate[l]), 2, axis=-1)
        merged = g_a * (y_a @ w_proj_a[l]) + g_b * (y_b @ w_proj_b[l])
        x = x + g1 * (merged @ w_out[l])

        h2 = modulate(rmsnorm(x, norm2_g[l]), sh2, sc2)
        x = x + g2 * expert_choice_ffn(h2, w_router[l], w_exp_gate[l], w_exp_up[l], w_exp_down[l])

    return rmsnorm(x, final_g)
```

```python
import functools

import jax
import jax.numpy as jnp
from jax import lax
from jax.experimental import pallas as pl
from jax.experimental.pallas import tpu as pltpu

F32 = jnp.float32
BF16 = jnp.bfloat16

D_MODEL = 1024
SEQ = 4096
CTX_LEN = 256
GRID_W = 64
EPS = 1e-6
CHUNK = 128
A_GROUPS = 8
A_WIDTH = 1024
N_HEADS = 16
N_KV_HEADS = 4
HEAD_DIM = 64
B_WIDTH = N_HEADS * HEAD_DIM
KV_WIDTH = N_KV_HEADS * HEAD_DIM
WINDOW = 128
BLOCK = 128
ROPE_BASE = 10000.0
N_EXPERTS = 16
CAPACITY = 2 * SEQ // N_EXPERTS
D_EXPERT = 2048
OFF_U = 0
OFF_VA = OFF_U + A_WIDTH
OFF_Q = OFF_VA + A_WIDTH
OFF_K = OFF_Q + B_WIDTH
OFF_VB = OFF_K + KV_WIDTH
OFF_GATE = OFF_VB + KV_WIDTH
IN_COLS = OFF_GATE + 2 * D_MODEL

LANES = 128
KV_DUP_WIDTH = N_KV_HEADS * LANES
VMEM_LIMIT = 56 * 1024 * 1024
MASK_VALUE = -1e30

TOK_TILE = 512
COMB_TILE = 1024
CUMSUM_BLOCK = 512


def _params(*sem):
    return pltpu.CompilerParams(dimension_semantics=sem, vmem_limit_bytes=VMEM_LIMIT)


def _gelu_tanh(x):
    return 0.5 * x * (1.0 + jnp.tanh(0.7978845608028654 * (x + 0.044715 * (x * x * x))))


def _rms_modulate(x, gain, shift, scale):
    y = x * lax.rsqrt(jnp.mean(x * x, axis=-1, keepdims=True) + EPS)
    return (y * gain) * (1.0 + scale) + shift


def _dup_kv_heads(x):
    lane = lax.broadcasted_iota(jnp.int32, x.shape, 1)
    lo = lane < HEAD_DIM
    r = pltpu.roll(x, HEAD_DIM, axis=1)
    return jnp.where(lo, x, r), jnp.where(lo, r, x)


def _ada_kernel(c_ref, w_ref, b_ref, o_ref):
    c = c_ref[...]
    s = c * jax.nn.sigmoid(c)
    o_ref[...] = jnp.dot(s, w_ref[...], preferred_element_type=F32,
                         precision=lax.Precision.HIGHEST) + b_ref[...]


def _ada(cc, w, b):
    rows, d = cc.shape
    cols = w.shape[1]
    tn = 1536
    return pl.pallas_call(
        _ada_kernel,
        grid=(cols // tn,),
        in_specs=[pl.BlockSpec((rows, d), lambda j: (0, 0)),
                  pl.BlockSpec((d, tn), lambda j: (0, j)),
                  pl.BlockSpec((1, tn), lambda j: (0, j))],
        out_specs=pl.BlockSpec((rows, tn), lambda j: (0, j)),
        out_shape=jax.ShapeDtypeStruct((rows, cols), F32),
        compiler_params=_params("arbitrary"),
        name="ada",
    )(cc, w, b)


def _inproj_kernel(x_ref, sh_ref, sc_ref, g_ref, w_ref, bg_ref, lng_ref, lnb_ref, ws_ref, bs_ref,
                   cos_ref, sin_ref, ya_ref, q_ref, kd_ref, vd_ref, ga_ref, gb_ref):
    tm = x_ref.shape[1]
    h = _rms_modulate(x_ref[0], g_ref[...], sh_ref[0], sc_ref[0]).astype(BF16)

    def proj(lo, hi):
        return jnp.dot(h, w_ref[:, lo:hi], preferred_element_type=F32)

    u = _gelu_tanh(proj(OFF_U, OFF_VA))
    v = _gelu_tanh(proj(OFF_VA, OFF_Q))
    mu = jnp.mean(v, axis=-1, keepdims=True)
    vc = v - mu
    var = jnp.mean(vc * vc, axis=-1, keepdims=True)
    vln = ((vc * lax.rsqrt(var + EPS)) * lng_ref[...] + lnb_ref[...]).astype(BF16)
    nc = tm // CHUNK
    gd = A_WIDTH // A_GROUPS
    for g in range(A_GROUPS):
        rhs = jnp.concatenate(
            [vln[c * CHUNK:(c + 1) * CHUNK, g * gd:(g + 1) * gd] for c in range(nc)], axis=1)
        mixed = jnp.dot(ws_ref[g], rhs, preferred_element_type=F32) + bs_ref[:, g:g + 1]
        for c in range(nc):
            ya_ref[0, c * CHUNK:(c + 1) * CHUNK, g * gd:(g + 1) * gd] = (
                u[c * CHUNK:(c + 1) * CHUNK, g * gd:(g + 1) * gd]
                * mixed[:, c * CHUNK:(c + 1) * CHUNK]).astype(BF16)

    cos = cos_ref[...]
    sin = sin_ref[...]
    lane = lax.broadcasted_iota(jnp.int32, (tm, LANES), 1)
    first_half = (lane % 32) < 16

    def rope(t):
        partner = jnp.where(first_half, pltpu.roll(t, LANES - 16, axis=1), pltpu.roll(t, 16, axis=1))
        return t * cos + partner * sin

    q = proj(OFF_Q, OFF_K)
    for j in range(B_WIDTH // LANES):
        q_ref[0, :, j * LANES:(j + 1) * LANES] = (
            rope(q[:, j * LANES:(j + 1) * LANES]) * (HEAD_DIM ** -0.5)).astype(BF16)
    k = proj(OFF_K, OFF_VB)
    vb = proj(OFF_VB, OFF_GATE)
    for j in range(KV_WIDTH // LANES):
        ke, ko = _dup_kv_heads(rope(k[:, j * LANES:(j + 1) * LANES]))
        kd_ref[0, :, (2 * j) * LANES:(2 * j + 1) * LANES] = ke.astype(BF16)
        kd_ref[0, :, (2 * j + 1) * LANES:(2 * j + 2) * LANES] = ko.astype(BF16)
        ve, vo = _dup_kv_heads(vb[:, j * LANES:(j + 1) * LANES])
        vd_ref[0, :, (2 * j) * LANES:(2 * j + 1) * LANES] = ve.astype(BF16)
        vd_ref[0, :, (2 * j + 1) * LANES:(2 * j + 2) * LANES] = vo.astype(BF16)

    ga_ref[0] = jax.nn.sigmoid(proj(OFF_GATE, OFF_GATE + D_MODEL) + bg_ref[:, :D_MODEL]).astype(BF16)
    gb_ref[0] = jax.nn.sigmoid(proj(OFF_GATE + D_MODEL, IN_COLS) + bg_ref[:, D_MODEL:]).astype(BF16)


def _inproj(x, sh1, sc1, g1n, w_in, b_gate, ln_g, ln_b, w_s, b_s_t, cos_t, sin_t):
    bn, L, d = x.shape
    tm = TOK_TILE
    tok = lambda b, i: (b, i, 0)
    per_b = lambda b, i: (b, 0, 0)
    const2 = lambda b, i: (0, 0)
    out = lambda w: jax.ShapeDtypeStruct((bn, L, w), BF16)
    return pl.pallas_call(
        _inproj_kernel,
        grid=(bn, L // tm),
        in_specs=[pl.BlockSpec((1, tm, d), tok),
                  pl.BlockSpec((1, 1, d), per_b),
                  pl.BlockSpec((1, 1, d), per_b),
                  pl.BlockSpec((1, d), const2),
                  pl.BlockSpec((d, IN_COLS), const2),
                  pl.BlockSpec((1, 2 * d), const2),
                  pl.BlockSpec((1, A_WIDTH), const2),
                  pl.BlockSpec((1, A_WIDTH), const2),
                  pl.BlockSpec((A_GROUPS, CHUNK, CHUNK), lambda b, i: (0, 0, 0)),
                  pl.BlockSpec((CHUNK, A_GROUPS), const2),
                  pl.BlockSpec((tm, LANES), lambda b, i: (i, 0)),
                  pl.BlockSpec((tm, LANES), lambda b, i: (i, 0))],
        out_specs=[pl.BlockSpec((1, tm, A_WIDTH), tok),
                   pl.BlockSpec((1, tm, B_WIDTH), tok),
                   pl.BlockSpec((1, tm, KV_DUP_WIDTH), tok),
                   pl.BlockSpec((1, tm, KV_DUP_WIDTH), tok),
                   pl.BlockSpec((1, tm, d), tok),
                   pl.BlockSpec((1, tm, d), tok)],
        out_shape=[out(A_WIDTH), out(B_WIDTH), out(KV_DUP_WIDTH), out(KV_DUP_WIDTH), out(d), out(d)],
        compiler_params=_params("arbitrary", "arbitrary"),
        name="inproj",
    )(x, sh1, sc1, g1n, w_in, b_gate, ln_g, ln_b, w_s, b_s_t, cos_t, sin_t)


def _ctxkv_kernel(x_ref, sh_ref, sc_ref, g_ref, w_ref, kd_ref, vd_ref):
    h = _rms_modulate(x_ref[0], g_ref[...], sh_ref[...], sc_ref[...]).astype(BF16)
    kv = jnp.dot(h, w_ref[...], preferred_element_type=F32)
    for j in range(KV_WIDTH // LANES):
        ke, ko = _dup_kv_heads(kv[:, j * LANES:(j + 1) * LANES])
        kd_ref[0, :, (2 * j) * LANES:(2 * j + 1) * LANES] = ke.astype(BF16)
        kd_ref[0, :, (2 * j + 1) * LANES:(2 * j + 2) * LANES] = ko.astype(BF16)
        ve, vo = _dup_kv_heads(kv[:, KV_WIDTH + j * LANES:KV_WIDTH + (j + 1) * LANES])
        vd_ref[0, :, (2 * j) * LANES:(2 * j + 1) * LANES] = ve.astype(BF16)
        vd_ref[0, :, (2 * j + 1) * LANES:(2 * j + 2) * LANES] = vo.astype(BF16)


def _ctxkv(ctx, csh1, csc1, g1n, w_kv):
    bn, n, d = ctx.shape
    const2 = lambda b: (0, 0)
    blk = lambda b: (b, 0, 0)
    return pl.pallas_call(
        _ctxkv_kernel,
        grid=(bn,),
        in_specs=[pl.BlockSpec((1, n, d), blk),
                  pl.BlockSpec((1, d), const2),
                  pl.BlockSpec((1, d), const2),
                  pl.BlockSpec((1, d), const2),
                  pl.BlockSpec((d, 2 * KV_WIDTH), const2)],
        out_specs=[pl.BlockSpec((1, n, KV_DUP_WIDTH), blk), pl.BlockSpec((1, n, KV_DUP_WIDTH), blk)],
        out_shape=[jax.ShapeDtypeStruct((bn, n, KV_DUP_WIDTH), BF16)] * 2,
        compiler_params=_params("arbitrary"),
        name="ctxkv",
    )(ctx, csh1, csc1, g1n, w_kv)


def _attn_kernel(sink_ref, q_ref, kp_ref, kc_ref, kn_ref, vp_ref, vc_ref, vn_ref, kx_ref, vx_ref, o_ref):
    n = pl.program_id(1)
    rep = N_HEADS // N_KV_HEADS
    rows = rep * BLOCK
    n_loc = 3 * BLOCK
    cols = n_loc + CTX_LEN
    r = lax.broadcasted_iota(jnp.int32, (rows, cols), 0)
    s = lax.broadcasted_iota(jnp.int32, (rows, cols), 1)
    rel = s - BLOCK - (r % BLOCK)
    key_pos = (n - 1) * BLOCK + s
    valid = ((jnp.abs(rel) <= WINDOW) & (key_pos >= 0) & (key_pos < SEQ)) | (s >= n_loc)
    lo = lax.broadcasted_iota(jnp.int32, (BLOCK, LANES), 1) < HEAD_DIM
    rcol = lax.broadcasted_iota(jnp.int32, (rows, 1), 0)
    zero = jnp.zeros((BLOCK, LANES), BF16)

    for kh in range(N_KV_HEADS):
        ls = slice(kh * LANES, (kh + 1) * LANES)
        parts = []
        for j in (2 * kh, 2 * kh + 1):
            qc = q_ref[0, :, j * LANES:(j + 1) * LANES]
            parts += [jnp.where(lo, qc, zero), jnp.where(lo, zero, qc)]
        lhs = jnp.concatenate(parts, axis=0)
        k_all = jnp.concatenate([kp_ref[0, :, ls], kc_ref[0, :, ls], kn_ref[0, :, ls], kx_ref[0, :, ls]], axis=0)
        v_all = jnp.concatenate([vp_ref[0, :, ls], vc_ref[0, :, ls], vn_ref[0, :, ls], vx_ref[0, :, ls]], axis=0)
        logits = lax.dot_general(lhs, k_all, (((1,), (1,)), ((), ())), preferred_element_type=F32)
        logits = jnp.where(valid, logits, MASK_VALUE)
        sink = jnp.where(rcol < BLOCK, sink_ref[rep * kh],
                         jnp.where(rcol < 2 * BLOCK, sink_ref[rep * kh + 1],
                                   jnp.where(rcol < 3 * BLOCK, sink_ref[rep * kh + 2], sink_ref[rep * kh + 3])))
        m = jnp.maximum(jnp.max(logits, axis=1, keepdims=True), sink)
        p = jnp.exp(logits - m)
        denom = jnp.sum(p, axis=1, keepdims=True) + jnp.exp(sink - m)
        o = jnp.dot(p.astype(BF16), v_all, preferred_element_type=F32) / denom
        o_ref[0, :, (2 * kh) * LANES:(2 * kh + 1) * LANES] = jnp.where(
            lo, o[0:BLOCK], o[BLOCK:2 * BLOCK]).astype(BF16)
        o_ref[0, :, (2 * kh + 1) * LANES:(2 * kh + 2) * LANES] = jnp.where(
            lo, o[2 * BLOCK:3 * BLOCK], o[3 * BLOCK:4 * BLOCK]).astype(BF16)


def _attn(sink, q, kd, vd, kxd, vxd):
    bn, L, _ = q.shape
    nb = L // BLOCK
    cur = lambda b, n: (b, n, 0)
    prev = lambda b, n: (b, jnp.maximum(n - 1, 0), 0)
    nxt = lambda b, n: (b, jnp.minimum(n + 1, nb - 1), 0)
    per_b = lambda b, n: (b, 0, 0)
    kv_blk = (1, BLOCK, KV_DUP_WIDTH)
    return pl.pallas_call(
        _attn_kernel,
        grid=(bn, nb),
        in_specs=[pl.BlockSpec(memory_space=pltpu.SMEM),
                  pl.BlockSpec((1, BLOCK, B_WIDTH), cur),
                  pl.BlockSpec(kv_blk, prev), pl.BlockSpec(kv_blk, cur), pl.BlockSpec(kv_blk, nxt),
                  pl.BlockSpec(kv_blk, prev), pl.BlockSpec(kv_blk, cur), pl.BlockSpec(kv_blk, nxt),
                  pl.BlockSpec((1, CTX_LEN, KV_DUP_WIDTH), per_b),
                  pl.BlockSpec((1, CTX_LEN, KV_DUP_WIDTH), per_b)],
        out_specs=pl.BlockSpec((1, BLOCK, B_WIDTH), cur),
        out_shape=jax.ShapeDtypeStruct((bn, L, B_WIDTH), BF16),
        compiler_params=_params("arbitrary", "arbitrary"),
        name="attn",
    )(sink, q, kd, kd, kd, vd, vd, vd, kxd, vxd)


def _merge_kernel(x_ref, ya_ref, yb_ref, ga_ref, gb_ref, wa_ref, wb_ref, wo_ref, g1_ref, sh_ref, sc_ref,
                  gn_ref, wr_ref, x1_ref, h2_ref, aff_ref):
    pa = jnp.dot(ya_ref[0], wa_ref[...], preferred_element_type=F32)
    pb = jnp.dot(yb_ref[0], wb_ref[...], preferred_element_type=F32)
    merged = (ga_ref[0].astype(F32) * pa + gb_ref[0].astype(F32) * pb).astype(BF16)
    x1 = x_ref[0] + g1_ref[0] * jnp.dot(merged, wo_ref[...], preferred_element_type=F32)
    x1_ref[0] = x1
    h2 = _rms_modulate(x1, gn_ref[...], sh_ref[0], sc_ref[0]).astype(BF16)
    h2_ref[0] = h2
    logits = jnp.dot(h2, wr_ref[...], preferred_element_type=F32)
    e = jnp.exp(logits - jnp.max(logits, axis=-1, keepdims=True))
    aff_ref[0] = e / jnp.sum(e, axis=-1, keepdims=True)


def _merge(x, ya, yb, ga, gb, wa, wb, wo, g1, sh2, sc2, g2n, wr):
    bn, L, d = x.shape
    tm = TOK_TILE
    tok = lambda b, i: (b, i, 0)
    per_b = lambda b, i: (b, 0, 0)
    const2 = lambda b, i: (0, 0)
    return pl.pallas_call(
        _merge_kernel,
        grid=(bn, L // tm),
        in_specs=[pl.BlockSpec((1, tm, d), tok)] * 5
        + [pl.BlockSpec((d, d), const2)] * 3
        + [pl.BlockSpec((1, 1, d), per_b)] * 3
        + [pl.BlockSpec((1, d), const2), pl.BlockSpec((d, N_EXPERTS), const2)],
        out_specs=[pl.BlockSpec((1, tm, d), tok), pl.BlockSpec((1, tm, d), tok),
                   pl.BlockSpec((1, tm, N_EXPERTS), tok)],
        out_shape=[jax.ShapeDtypeStruct((bn, L, d), F32), jax.ShapeDtypeStruct((bn, L, d), BF16),
                   jax.ShapeDtypeStruct((bn, L, N_EXPERTS), F32)],
        compiler_params=_params("arbitrary", "arbitrary"),
        name="merge",
    )(x, ya, yb, ga, gb, wa, wb, wo, g1, sh2, sc2, g2n, wr)


def _excl_prefix_count(m, upper):
    n = m.shape[1]
    outs = []
    carry = jnp.zeros((m.shape[0], 1), F32)
    for blk in range(n // CUMSUM_BLOCK):
        mb = m[:, blk * CUMSUM_BLOCK:(blk + 1) * CUMSUM_BLOCK]
        outs.append(jnp.dot(mb, upper, preferred_element_type=F32) + carry)
        carry = carry + jnp.sum(mb.astype(F32), axis=1, keepdims=True)
    return jnp.concatenate(outs, axis=1)


def _select_kernel(aff_ref, slot_ref):
    x = aff_ref[0]
    one = jnp.ones(x.shape, F32)
    zero = jnp.zeros(x.shape, F32)

    def count(mask):
        return jnp.sum(jnp.where(mask, one, zero), axis=1, keepdims=True)

    def body(i, t):
        cand = t | jnp.left_shift(jnp.int32(1), 30 - i)
        return jnp.where(count(x >= lax.bitcast_convert_type(cand, F32)) >= CAPACITY, cand, t)

    thr = lax.bitcast_convert_type(
        lax.fori_loop(0, 31, body, jnp.zeros((x.shape[0], 1), jnp.int32)), F32)
    gt = x > thr
    eq = x == thr
    need = CAPACITY - count(gt)
    ri = lax.broadcasted_iota(jnp.int32, (CUMSUM_BLOCK, CUMSUM_BLOCK), 0)
    ci = lax.broadcasted_iota(jnp.int32, (CUMSUM_BLOCK, CUMSUM_BLOCK), 1)
    upper = jnp.where(ri < ci, 1.0, 0.0).astype(BF16)
    eq_rank = _excl_prefix_count(jnp.where(eq, one, zero).astype(BF16), upper)
    sel = gt | (eq & (eq_rank < need))
    slot = _excl_prefix_count(jnp.where(sel, one, zero).astype(BF16), upper)
    slot_ref[0] = jnp.where(sel, slot, -1.0).astype(jnp.int32)


def _select(aff_t):
    bn, e, n = aff_t.shape
    blk = lambda b: (b, 0, 0)
    return pl.pallas_call(
        _select_kernel,
        grid=(bn,),
        in_specs=[pl.BlockSpec((1, e, n), blk)],
        out_specs=pl.BlockSpec((1, e, n), blk),
        out_shape=jax.ShapeDtypeStruct((bn, e, n), jnp.int32),
        compiler_params=_params("arbitrary"),
        name="select",
    )(aff_t)


def _gather_kernel(slot_ref, h_ref, xe_ref):
    n = h_ref.shape[1]
    srow = lax.broadcasted_iota(jnp.int32, (CAPACITY, n), 0)
    onehot = jnp.where(srow == slot_ref[0, 0], 1.0, 0.0).astype(BF16)
    xe_ref[0, 0] = jnp.dot(onehot, h_ref[0], preferred_element_type=F32).astype(BF16)


def _gather(slot4, h2):
    bn, n, d = h2.shape
    return pl.pallas_call(
        _gather_kernel,
        grid=(bn, N_EXPERTS),
        in_specs=[pl.BlockSpec((1, 1, 1, n), lambda b, e: (b, e, 0, 0)),
                  pl.BlockSpec((1, n, d), lambda b, e: (b, 0, 0))],
        out_specs=pl.BlockSpec((1, 1, CAPACITY, d), lambda b, e: (b, e, 0, 0)),
        out_shape=jax.ShapeDtypeStruct((bn, N_EXPERTS, CAPACITY, d), BF16),
        compiler_params=_params("arbitrary", "arbitrary"),
        name="gather",
    )(slot4, h2)


def _ffn_kernel(xe_ref, wg_ref, wu_ref, wd_ref, y_ref):
    xe = xe_ref[0, 0]
    fc = 512
    acc = jnp.zeros((xe.shape[0], wd_ref.shape[2]), F32)
    for j in range(D_EXPERT // fc):
        g = jnp.dot(xe, wg_ref[0, :, j * fc:(j + 1) * fc], preferred_element_type=F32)
        u = jnp.dot(xe, wu_ref[0, :, j * fc:(j + 1) * fc], preferred_element_type=F32)
        a = ((g * jax.nn.sigmoid(g)) * u).astype(BF16)
        acc = acc + jnp.dot(a, wd_ref[0, j * fc:(j + 1) * fc, :], preferred_element_type=F32)
    y_ref[0, 0] = acc.astype(BF16)


def _ffn(xe, wg, wu, wd):
    bn, ne, c, d = xe.shape
    f = wg.shape[2]
    return pl.pallas_call(
        _ffn_kernel,
        grid=(ne, bn),
        in_specs=[pl.BlockSpec((1, 1, c, d), lambda e, b: (b, e, 0, 0)),
                  pl.BlockSpec((1, d, f), lambda e, b: (e, 0, 0)),
                  pl.BlockSpec((1, d, f), lambda e, b: (e, 0, 0)),
                  pl.BlockSpec((1, f, d), lambda e, b: (e, 0, 0))],
        out_specs=pl.BlockSpec((1, 1, c, d), lambda e, b: (b, e, 0, 0)),
        out_shape=jax.ShapeDtypeStruct((bn, ne, c, d), BF16),
        compiler_params=_params("arbitrary", "arbitrary"),
        name="ffn",
    )(xe, wg, wu, wd)


def _combine_kernel(slot_ref, aff_ref, y_ref, x1_ref, g2_ref, gf_ref, o_ref, acc_ref):
    e = pl.program_id(2)

    @pl.when(e == 0)
    def _():
        acc_ref[...] = jnp.zeros_like(acc_ref)

    tile = slot_ref.shape[1]
    lane = lax.broadcasted_iota(jnp.int32, (tile, N_EXPERTS), 1)
    mine = lane == e
    slot_col = jnp.sum(jnp.where(mine, slot_ref[0].astype(F32), 0.0), axis=1, keepdims=True)
    w_col = jnp.sum(jnp.where(mine, aff_ref[0], 0.0), axis=1, keepdims=True)
    scol = lax.broadcasted_iota(jnp.int32, (tile, CAPACITY), 1).astype(F32)
    onehot = jnp.where(scol == slot_col, 1.0, 0.0).astype(BF16)
    acc_ref[...] += w_col * jnp.dot(onehot, y_ref[0, 0], preferred_element_type=F32)

    @pl.when(e == N_EXPERTS - 1)
    def _():
        xo = x1_ref[0] + g2_ref[0] * acc_ref[...]
        o_ref[0] = (xo * lax.rsqrt(jnp.mean(xo * xo, axis=-1, keepdims=True) + EPS)) * gf_ref[...]


def _combine(slot_t, aff, y, x1, g2, gf):
    bn, n, d = x1.shape
    tile = COMB_TILE
    tok = lambda b, i, e: (b, i, 0)
    return pl.pallas_call(
        _combine_kernel,
        grid=(bn, n // tile, N_EXPERTS),
        in_specs=[pl.BlockSpec((1, tile, N_EXPERTS), tok),
                  pl.BlockSpec((1, tile, N_EXPERTS), tok),
                  pl.BlockSpec((1, 1, CAPACITY, d), lambda b, i, e: (b, e, 0, 0)),
                  pl.BlockSpec((1, tile, d), tok),
                  pl.BlockSpec((1, 1, d), lambda b, i, e: (b, 0, 0)),
                  pl.BlockSpec((1, d), lambda b, i, e: (0, 0))],
        out_specs=pl.BlockSpec((1, tile, d), tok),
        out_shape=jax.ShapeDtypeStruct((bn, n, d), F32),
        scratch_shapes=[pltpu.VMEM((tile, d), F32)],
        compiler_params=_params("arbitrary", "arbitrary", "arbitrary"),
        name="combine",
    )(slot_t, aff, y, x1, g2, gf)


def _rope_tables(L):
    pos = jnp.arange(L)
    row = (pos // GRID_W).astype(F32)
    col = (pos % GRID_W).astype(F32)
    half = HEAD_DIM // 2
    freqs = ROPE_BASE ** (-jnp.arange(0, half, 2, dtype=F32) / half)
    ang_r = row[:, None] * freqs[None, :]
    ang_c = col[:, None] * freqs[None, :]
    cos_h = jnp.concatenate([jnp.cos(ang_r), jnp.cos(ang_r), jnp.cos(ang_c), jnp.cos(ang_c)], axis=1)
    sin_h = jnp.concatenate([-jnp.sin(ang_r), jnp.sin(ang_r), -jnp.sin(ang_c), jnp.sin(ang_c)], axis=1)
    return jnp.tile(cos_h, (1, LANES // HEAD_DIM)), jnp.tile(sin_h, (1, LANES // HEAD_DIM))


def kernel(x, c, ctx, c_ctx, w_ada, b_ada, norm1_g, w_in, b_merge_gate, gmlp_ln_g, gmlp_ln_b, w_spatial,
           b_spatial, attn_sink, w_proj_a, w_proj_b, w_out, norm2_g, w_router, w_exp_gate, w_exp_up,
           w_exp_down, final_g):
    bn, L, d = x.shape
    depth = w_ada.shape[0]
    assert depth == 1, "the combine kernel fuses the final rmsnorm, so it must follow the only layer"
    cos_t, sin_t = _rope_tables(L)
    ada_rows = 16
    cc = jnp.zeros((ada_rows, d), F32).at[:bn].set(c).at[bn].set(c_ctx)

    for l in range(depth):
        mod = _ada(cc, w_ada[l], b_ada[l][None, :])
        sh1, sc1, g1, sh2, sc2, g2 = [mod[:bn, None, i * d:(i + 1) * d] for i in range(6)]
        csh1 = mod[bn:bn + 1, 0:d]
        csc1 = mod[bn:bn + 1, d:2 * d]
        w_in_b = w_in[l].astype(BF16)
        g1n = norm1_g[l][None, :]

        ya, q, kd, vd, ga, gb = _inproj(
            x, sh1, sc1, g1n, w_in_b, b_merge_gate[l][None, :], gmlp_ln_g[l][None, :],
            gmlp_ln_b[l][None, :], w_spatial[l].astype(BF16), b_spatial[l].T, cos_t, sin_t)
        kxd, vxd = _ctxkv(ctx, csh1, csc1, g1n, w_in_b[:, OFF_K:OFF_GATE])
        yb = _attn(attn_sink[l], q, kd, vd, kxd, vxd)

        x1, h2, aff = _merge(x, ya, yb, ga, gb, w_proj_a[l].astype(BF16), w_proj_b[l].astype(BF16),
                             w_out[l].astype(BF16), g1, sh2, sc2, norm2_g[l][None, :],
                             w_router[l].astype(BF16))
        slot = _select(jnp.swapaxes(aff, 1, 2))
        xe = _gather(slot[:, :, None, :], h2)
        y = _ffn(xe, w_exp_gate[l].astype(BF16), w_exp_up[l].astype(BF16), w_exp_down[l].astype(BF16))
        x = _combine(jnp.swapaxes(slot, 1, 2), aff, y, x1, g2, final_g[None, :])
    return x
```

```python
import functools

import jax
import jax.numpy as jnp
from jax import lax
from jax.experimental import pallas as pl
from jax.experimental.pallas import tpu as pltpu

F32 = jnp.float32
BF16 = jnp.bfloat16

D_MODEL = 1024
SEQ = 4096
CTX_LEN = 256
GRID_W = 64
EPS = 1e-6
CHUNK = 128
A_GROUPS = 8
A_WIDTH = 1024
N_HEADS = 16
N_KV_HEADS = 4
HEAD_DIM = 64
B_WIDTH = N_HEADS * HEAD_DIM
KV_WIDTH = N_KV_HEADS * HEAD_DIM
WINDOW = 128
BLOCK = 128
ROPE_BASE = 10000.0
N_EXPERTS = 16
CAPACITY = 2 * SEQ // N_EXPERTS
D_EXPERT = 2048
OFF_U = 0
OFF_VA = OFF_U + A_WIDTH
OFF_Q = OFF_VA + A_WIDTH
OFF_K = OFF_Q + B_WIDTH
OFF_VB = OFF_K + KV_WIDTH
OFF_GATE = OFF_VB + KV_WIDTH
IN_COLS = OFF_GATE + 2 * D_MODEL

LANES = 128
KV_DUP_WIDTH = N_KV_HEADS * LANES
VMEM_LIMIT = 56 * 1024 * 1024
MASK_VALUE = -1e30

TOK_TILE = 512
WIN_TOK_TILE = 512
WIN_SLOTS = 128
BF16_SUBLANES = 16
CUMSUM_BLOCK = 512


def _params(*sem):
    return pltpu.CompilerParams(dimension_semantics=sem, vmem_limit_bytes=VMEM_LIMIT)


def _gelu_tanh(x):
    return 0.5 * x * (1.0 + jnp.tanh(0.7978845608028654 * (x + 0.044715 * (x * x * x))))


def _rms_modulate(x, gain, shift, scale):
    y = x * lax.rsqrt(jnp.mean(x * x, axis=-1, keepdims=True) + EPS)
    return (y * gain) * (1.0 + scale) + shift


def _dup_kv_heads(x):
    lane = lax.broadcasted_iota(jnp.int32, x.shape, 1)
    lo = lane < HEAD_DIM
    r = pltpu.roll(x, HEAD_DIM, axis=1)
    return jnp.where(lo, x, r), jnp.where(lo, r, x)


def _ada_kernel(c_ref, w_ref, b_ref, o_ref):
    c = c_ref[...]
    s = c * jax.nn.sigmoid(c)
    o_ref[...] = jnp.dot(s, w_ref[...], preferred_element_type=F32,
                         precision=lax.Precision.HIGHEST) + b_ref[...]


def _ada(cc, w, b):
    rows, d = cc.shape
    cols = w.shape[1]
    tn = 1536
    return pl.pallas_call(
        _ada_kernel,
        grid=(cols // tn,),
        in_specs=[pl.BlockSpec((rows, d), lambda j: (0, 0)),
                  pl.BlockSpec((d, tn), lambda j: (0, j)),
                  pl.BlockSpec((1, tn), lambda j: (0, j))],
        out_specs=pl.BlockSpec((rows, tn), lambda j: (0, j)),
        out_shape=jax.ShapeDtypeStruct((rows, cols), F32),
        compiler_params=_params("arbitrary"),
        name="ada",
    )(cc, w, b)


def _inproj_kernel(x_ref, sh_ref, sc_ref, g_ref, w_ref, bg_ref, lng_ref, lnb_ref, ws_ref, bs_ref,
                   cos_ref, sin_ref, ya_ref, q_ref, kd_ref, vd_ref, ga_ref, gb_ref):
    tm = x_ref.shape[1]
    h = _rms_modulate(x_ref[0], g_ref[...], sh_ref[0], sc_ref[0]).astype(BF16)

    def proj(lo, hi):
        return jnp.dot(h, w_ref[:, lo:hi], preferred_element_type=F32)

    u = _gelu_tanh(proj(OFF_U, OFF_VA))
    v = _gelu_tanh(proj(OFF_VA, OFF_Q))
    mu = jnp.mean(v, axis=-1, keepdims=True)
    vc = v - mu
    var = jnp.mean(vc * vc, axis=-1, keepdims=True)
    vln = ((vc * lax.rsqrt(var + EPS)) * lng_ref[...] + lnb_ref[...]).astype(BF16)
    nc = tm // CHUNK
    gd = A_WIDTH // A_GROUPS
    for g in range(A_GROUPS):
        rhs = jnp.concatenate(
            [vln[c * CHUNK:(c + 1) * CHUNK, g * gd:(g + 1) * gd] for c in range(nc)], axis=1)
        mixed = jnp.dot(ws_ref[g], rhs, preferred_element_type=F32) + bs_ref[:, g:g + 1]
        for c in range(nc):
            ya_ref[0, c * CHUNK:(c + 1) * CHUNK, g * gd:(g + 1) * gd] = (
                u[c * CHUNK:(c + 1) * CHUNK, g * gd:(g + 1) * gd]
                * mixed[:, c * CHUNK:(c + 1) * CHUNK]).astype(BF16)

    cos = cos_ref[...]
    sin = sin_ref[...]
    lane = lax.broadcasted_iota(jnp.int32, (tm, LANES), 1)
    first_half = (lane % 32) < 16

    def rope(t):
        partner = jnp.where(first_half, pltpu.roll(t, LANES - 16, axis=1), pltpu.roll(t, 16, axis=1))
        return t * cos + partner * sin

    q = proj(OFF_Q, OFF_K)
    for j in range(B_WIDTH // LANES):
        q_ref[0, :, j * LANES:(j + 1) * LANES] = (
            rope(q[:, j * LANES:(j + 1) * LANES]) * (HEAD_DIM ** -0.5)).astype(BF16)
    k = proj(OFF_K, OFF_VB)
    vb = proj(OFF_VB, OFF_GATE)
    for j in range(KV_WIDTH // LANES):
        ke, ko = _dup_kv_heads(rope(k[:, j * LANES:(j + 1) * LANES]))
        kd_ref[0, :, (2 * j) * LANES:(2 * j + 1) * LANES] = ke.astype(BF16)
        kd_ref[0, :, (2 * j + 1) * LANES:(2 * j + 2) * LANES] = ko.astype(BF16)
        ve, vo = _dup_kv_heads(vb[:, j * LANES:(j + 1) * LANES])
        vd_ref[0, :, (2 * j) * LANES:(2 * j + 1) * LANES] = ve.astype(BF16)
        vd_ref[0, :, (2 * j + 1) * LANES:(2 * j + 2) * LANES] = vo.astype(BF16)

    ga_ref[0] = jax.nn.sigmoid(proj(OFF_GATE, OFF_GATE + D_MODEL) + bg_ref[:, :D_MODEL]).astype(BF16)
    gb_ref[0] = jax.nn.sigmoid(proj(OFF_GATE + D_MODEL, IN_COLS) + bg_ref[:, D_MODEL:]).astype(BF16)


def _inproj(x, sh1, sc1, g1n, w_in, b_gate, ln_g, ln_b, w_s, b_s_t, cos_t, sin_t):
    bn, L, d = x.shape
    tm = TOK_TILE
    tok = lambda b, i: (b, i, 0)
    per_b = lambda b, i: (b, 0, 0)
    const2 = lambda b, i: (0, 0)
    out = lambda w: jax.ShapeDtypeStruct((bn, L, w), BF16)
    return pl.pallas_call(
        _inproj_kernel,
        grid=(bn, L // tm),
        in_specs=[pl.BlockSpec((1, tm, d), tok),
                  pl.BlockSpec((1, 1, d), per_b),
                  pl.BlockSpec((1, 1, d), per_b),
                  pl.BlockSpec((1, d), const2),
                  pl.BlockSpec((d, IN_COLS), const2),
                  pl.BlockSpec((1, 2 * d), const2),
                  pl.BlockSpec((1, A_WIDTH), const2),
                  pl.BlockSpec((1, A_WIDTH), const2),
                  pl.BlockSpec((A_GROUPS, CHUNK, CHUNK), lambda b, i: (0, 0, 0)),
                  pl.BlockSpec((CHUNK, A_GROUPS), const2),
                  pl.BlockSpec((tm, LANES), lambda b, i: (i, 0)),
                  pl.BlockSpec((tm, LANES), lambda b, i: (i, 0))],
        out_specs=[pl.BlockSpec((1, tm, A_WIDTH), tok),
                   pl.BlockSpec((1, tm, B_WIDTH), tok),
                   pl.BlockSpec((1, tm, KV_DUP_WIDTH), tok),
                   pl.BlockSpec((1, tm, KV_DUP_WIDTH), tok),
                   pl.BlockSpec((1, tm, d), tok),
                   pl.BlockSpec((1, tm, d), tok)],
        out_shape=[out(A_WIDTH), out(B_WIDTH), out(KV_DUP_WIDTH), out(KV_DUP_WIDTH), out(d), out(d)],
        compiler_params=_params("arbitrary", "arbitrary"),
        name="inproj",
    )(x, sh1, sc1, g1n, w_in, b_gate, ln_g, ln_b, w_s, b_s_t, cos_t, sin_t)


def _ctxkv_kernel(x_ref, sh_ref, sc_ref, g_ref, w_ref, kd_ref, vd_ref):
    h = _rms_modulate(x_ref[0], g_ref[...], sh_ref[...], sc_ref[...]).astype(BF16)
    kv = jnp.dot(h, w_ref[...], preferred_element_type=F32)
    for j in range(KV_WIDTH // LANES):
        ke, ko = _dup_kv_heads(kv[:, j * LANES:(j + 1) * LANES])
        kd_ref[0, :, (2 * j) * LANES:(2 * j + 1) * LANES] = ke.astype(BF16)
        kd_ref[0, :, (2 * j + 1) * LANES:(2 * j + 2) * LANES] = ko.astype(BF16)
        ve, vo = _dup_kv_heads(kv[:, KV_WIDTH + j * LANES:KV_WIDTH + (j + 1) * LANES])
        vd_ref[0, :, (2 * j) * LANES:(2 * j + 1) * LANES] = ve.astype(BF16)
        vd_ref[0, :, (2 * j + 1) * LANES:(2 * j + 2) * LANES] = vo.astype(BF16)


def _ctxkv(ctx, csh1, csc1, g1n, w_kv):
    bn, n, d = ctx.shape
    const2 = lambda b: (0, 0)
    blk = lambda b: (b, 0, 0)
    return pl.pallas_call(
        _ctxkv_kernel,
        grid=(bn,),
        in_specs=[pl.BlockSpec((1, n, d), blk),
                  pl.BlockSpec((1, d), const2),
                  pl.BlockSpec((1, d), const2),
                  pl.BlockSpec((1, d), const2),
                  pl.BlockSpec((d, 2 * KV_WIDTH), const2)],
        out_specs=[pl.BlockSpec((1, n, KV_DUP_WIDTH), blk), pl.BlockSpec((1, n, KV_DUP_WIDTH), blk)],
        out_shape=[jax.ShapeDtypeStruct((bn, n, KV_DUP_WIDTH), BF16)] * 2,
        compiler_params=_params("arbitrary"),
        name="ctxkv",
    )(ctx, csh1, csc1, g1n, w_kv)


def _attn_kernel(sink_ref, q_ref, kp_ref, kc_ref, kn_ref, vp_ref, vc_ref, vn_ref, kx_ref, vx_ref, o_ref):
    n = pl.program_id(1)
    rep = N_HEADS // N_KV_HEADS
    rows = rep * BLOCK
    n_loc = 3 * BLOCK
    cols = n_loc + CTX_LEN
    r = lax.broadcasted_iota(jnp.int32, (rows, cols), 0)
    s = lax.broadcasted_iota(jnp.int32, (rows, cols), 1)
    rel = s - BLOCK - (r % BLOCK)
    key_pos = (n - 1) * BLOCK + s
    valid = ((jnp.abs(rel) <= WINDOW) & (key_pos >= 0) & (key_pos < SEQ)) | (s >= n_loc)
    lo = lax.broadcasted_iota(jnp.int32, (BLOCK, LANES), 1) < HEAD_DIM
    rcol = lax.broadcasted_iota(jnp.int32, (rows, 1), 0)
    zero = jnp.zeros((BLOCK, LANES), BF16)

    for kh in range(N_KV_HEADS):
        ls = slice(kh * LANES, (kh + 1) * LANES)
        parts = []
        for j in (2 * kh, 2 * kh + 1):
            qc = q_ref[0, :, j * LANES:(j + 1) * LANES]
            parts += [jnp.where(lo, qc, zero), jnp.where(lo, zero, qc)]
        lhs = jnp.concatenate(parts, axis=0)
        k_all = jnp.concatenate([kp_ref[0, :, ls], kc_ref[0, :, ls], kn_ref[0, :, ls], kx_ref[0, :, ls]], axis=0)
        v_all = jnp.concatenate([vp_ref[0, :, ls], vc_ref[0, :, ls], vn_ref[0, :, ls], vx_ref[0, :, ls]], axis=0)
        logits = lax.dot_general(lhs, k_all, (((1,), (1,)), ((), ())), preferred_element_type=F32)
        logits = jnp.where(valid, logits, MASK_VALUE)
        sink = jnp.where(rcol < BLOCK, sink_ref[rep * kh],
                         jnp.where(rcol < 2 * BLOCK, sink_ref[rep * kh + 1],
                                   jnp.where(rcol < 3 * BLOCK, sink_ref[rep * kh + 2], sink_ref[rep * kh + 3])))
        m = jnp.maximum(jnp.max(logits, axis=1, keepdims=True), sink)
        p = jnp.exp(logits - m)
        denom = jnp.sum(p, axis=1, keepdims=True) + jnp.exp(sink - m)
        o = jnp.dot(p.astype(BF16), v_all, preferred_element_type=F32) / denom
        o_ref[0, :, (2 * kh) * LANES:(2 * kh + 1) * LANES] = jnp.where(
            lo, o[0:BLOCK], o[BLOCK:2 * BLOCK]).astype(BF16)
        o_ref[0, :, (2 * kh + 1) * LANES:(2 * kh + 2) * LANES] = jnp.where(
            lo, o[2 * BLOCK:3 * BLOCK], o[3 * BLOCK:4 * BLOCK]).astype(BF16)


def _attn(sink, q, kd, vd, kxd, vxd):
    bn, L, _ = q.shape
    nb = L // BLOCK
    cur = lambda b, n: (b, n, 0)
    prev = lambda b, n: (b, jnp.maximum(n - 1, 0), 0)
    nxt = lambda b, n: (b, jnp.minimum(n + 1, nb - 1), 0)
    per_b = lambda b, n: (b, 0, 0)
    kv_blk = (1, BLOCK, KV_DUP_WIDTH)
    return pl.pallas_call(
        _attn_kernel,
        grid=(bn, nb),
        in_specs=[pl.BlockSpec(memory_space=pltpu.SMEM),
                  pl.BlockSpec((1, BLOCK, B_WIDTH), cur),
                  pl.BlockSpec(kv_blk, prev), pl.BlockSpec(kv_blk, cur), pl.BlockSpec(kv_blk, nxt),
                  pl.BlockSpec(kv_blk, prev), pl.BlockSpec(kv_blk, cur), pl.BlockSpec(kv_blk, nxt),
                  pl.BlockSpec((1, CTX_LEN, KV_DUP_WIDTH), per_b),
                  pl.BlockSpec((1, CTX_LEN, KV_DUP_WIDTH), per_b)],
        out_specs=pl.BlockSpec((1, BLOCK, B_WIDTH), cur),
        out_shape=jax.ShapeDtypeStruct((bn, L, B_WIDTH), BF16),
        compiler_params=_params("arbitrary", "arbitrary"),
        name="attn",
    )(sink, q, kd, kd, kd, vd, vd, vd, kxd, vxd)


def _merge_kernel(x_ref, ya_ref, yb_ref, ga_ref, gb_ref, wa_ref, wb_ref, wo_ref, g1_ref, sh_ref, sc_ref,
                  gn_ref, wr_ref, x1_ref, h2_ref, aff_ref):
    pa = jnp.dot(ya_ref[0], wa_ref[...], preferred_element_type=F32)
    pb = jnp.dot(yb_ref[0], wb_ref[...], preferred_element_type=F32)
    merged = (ga_ref[0].astype(F32) * pa + gb_ref[0].astype(F32) * pb).astype(BF16)
    x1 = x_ref[0] + g1_ref[0] * jnp.dot(merged, wo_ref[...], preferred_element_type=F32)
    x1_ref[0] = x1
    h2 = _rms_modulate(x1, gn_ref[...], sh_ref[0], sc_ref[0]).astype(BF16)
    h2_ref[0] = h2
    logits = jnp.dot(h2, wr_ref[...], preferred_element_type=F32)
    e = jnp.exp(logits - jnp.max(logits, axis=-1, keepdims=True))
    aff_ref[0] = e / jnp.sum(e, axis=-1, keepdims=True)


def _merge(x, ya, yb, ga, gb, wa, wb, wo, g1, sh2, sc2, g2n, wr):
    bn, L, d = x.shape
    tm = TOK_TILE
    tok = lambda b, i: (b, i, 0)
    per_b = lambda b, i: (b, 0, 0)
    const2 = lambda b, i: (0, 0)
    return pl.pallas_call(
        _merge_kernel,
        grid=(bn, L // tm),
        in_specs=[pl.BlockSpec((1, tm, d), tok)] * 5
        + [pl.BlockSpec((d, d), const2)] * 3
        + [pl.BlockSpec((1, 1, d), per_b)] * 3
        + [pl.BlockSpec((1, d), const2), pl.BlockSpec((d, N_EXPERTS), const2)],
        out_specs=[pl.BlockSpec((1, tm, d), tok), pl.BlockSpec((1, tm, d), tok),
                   pl.BlockSpec((1, tm, N_EXPERTS), tok)],
        out_shape=[jax.ShapeDtypeStruct((bn, L, d), F32), jax.ShapeDtypeStruct((bn, L, d), BF16),
                   jax.ShapeDtypeStruct((bn, L, N_EXPERTS), F32)],
        compiler_params=_params("arbitrary", "arbitrary"),
        name="merge",
    )(x, ya, yb, ga, gb, wa, wb, wo, g1, sh2, sc2, g2n, wr)


def _excl_prefix_count(m, upper):
    n = m.shape[1]
    outs = []
    carry = jnp.zeros((m.shape[0], 1), F32)
    for blk in range(n // CUMSUM_BLOCK):
        mb = m[:, blk * CUMSUM_BLOCK:(blk + 1) * CUMSUM_BLOCK]
        outs.append(jnp.dot(mb, upper, preferred_element_type=F32) + carry)
        carry = carry + jnp.sum(mb.astype(F32), axis=1, keepdims=True)
    return jnp.concatenate(outs, axis=1)


def _select_kernel(aff_ref, slot_ref, off_ref):
    x = aff_ref[0]
    one = jnp.ones(x.shape, F32)
    zero = jnp.zeros(x.shape, F32)

    def count(mask):
        return jnp.sum(jnp.where(mask, one, zero), axis=1, keepdims=True)

    def body(i, t):
        cand = t | jnp.left_shift(jnp.int32(1), 30 - i)
        return jnp.where(count(x >= lax.bitcast_convert_type(cand, F32)) >= CAPACITY, cand, t)

    thr = lax.bitcast_convert_type(
        lax.fori_loop(0, 31, body, jnp.zeros((x.shape[0], 1), jnp.int32)), F32)
    gt = x > thr
    eq = x == thr
    need = CAPACITY - count(gt)
    ri = lax.broadcasted_iota(jnp.int32, (CUMSUM_BLOCK, CUMSUM_BLOCK), 0)
    ci = lax.broadcasted_iota(jnp.int32, (CUMSUM_BLOCK, CUMSUM_BLOCK), 1)
    upper = jnp.where(ri < ci, 1.0, 0.0).astype(BF16)
    eq_rank = _excl_prefix_count(jnp.where(eq, one, zero).astype(BF16), upper)
    sel = gt | (eq & (eq_rank < need))
    slot = _excl_prefix_count(jnp.where(sel, one, zero).astype(BF16), upper)
    slot_ref[0] = jnp.where(sel, slot, -1.0).astype(jnp.int32)
    lane = lax.broadcasted_iota(jnp.int32, (x.shape[0], LANES), 1)
    off = jnp.zeros((x.shape[0], LANES), F32)
    for j in range(x.shape[1] // WIN_TOK_TILE):
        off = jnp.where(lane == j, slot[:, j * WIN_TOK_TILE:j * WIN_TOK_TILE + 1], off)
    off_ref[0] = off.astype(jnp.int32)


def _select(aff_t):
    bn, e, n = aff_t.shape
    blk = lambda b: (b, 0, 0)
    return pl.pallas_call(
        _select_kernel,
        grid=(bn,),
        in_specs=[pl.BlockSpec((1, e, n), blk)],
        out_specs=[pl.BlockSpec((1, e, n), blk), pl.BlockSpec((1, e, LANES), blk)],
        out_shape=[jax.ShapeDtypeStruct((bn, e, n), jnp.int32),
                   jax.ShapeDtypeStruct((bn, e, LANES), jnp.int32)],
        compiler_params=_params("arbitrary"),
        name="select",
    )(aff_t)


def _window_plan(off):
    n_tiles = SEQ // WIN_TOK_TILE
    first = off[:, :, :n_tiles]
    end = jnp.concatenate([first[:, :, 1:], jnp.full_like(first[:, :, :1], CAPACITY)], axis=2)
    start = jnp.minimum((first // BF16_SUBLANES) * BF16_SUBLANES, CAPACITY - WIN_SLOTS)
    overflow = jnp.any(end - start > WIN_SLOTS, axis=1).astype(jnp.int32)
    return jnp.swapaxes(start, 1, 2).reshape(-1), overflow.reshape(-1)


def _gather_kernel(slot_ref, h_ref, xe_ref):
    n = h_ref.shape[1]
    srow = lax.broadcasted_iota(jnp.int32, (CAPACITY, n), 0)
    onehot = jnp.where(srow == slot_ref[0, 0], 1.0, 0.0).astype(BF16)
    xe_ref[0, 0] = jnp.dot(onehot, h_ref[0], preferred_element_type=F32).astype(BF16)


def _gather(slot4, h2):
    bn, n, d = h2.shape
    return pl.pallas_call(
        _gather_kernel,
        grid=(bn, N_EXPERTS),
        in_specs=[pl.BlockSpec((1, 1, 1, n), lambda b, e: (b, e, 0, 0)),
                  pl.BlockSpec((1, n, d), lambda b, e: (b, 0, 0))],
        out_specs=pl.BlockSpec((1, 1, CAPACITY, d), lambda b, e: (b, e, 0, 0)),
        out_shape=jax.ShapeDtypeStruct((bn, N_EXPERTS, CAPACITY, d), BF16),
        compiler_params=_params("arbitrary", "arbitrary"),
        name="gather",
    )(slot4, h2)


def _ffn_kernel(xe_ref, wg_ref, wu_ref, wd_ref, y_ref):
    xe = xe_ref[0, 0]
    fc = 512
    acc = jnp.zeros((xe.shape[0], wd_ref.shape[2]), F32)
    for j in range(D_EXPERT // fc):
        g = jnp.dot(xe, wg_ref[0, :, j * fc:(j + 1) * fc], preferred_element_type=F32)
        u = jnp.dot(xe, wu_ref[0, :, j * fc:(j + 1) * fc], preferred_element_type=F32)
        a = ((g * jax.nn.sigmoid(g)) * u).astype(BF16)
        acc = acc + jnp.dot(a, wd_ref[0, j * fc:(j + 1) * fc, :], preferred_element_type=F32)
    y_ref[0, 0] = acc.astype(BF16)


def _ffn(xe, wg, wu, wd):
    bn, ne, c, d = xe.shape
    f = wg.shape[2]
    return pl.pallas_call(
        _ffn_kernel,
        grid=(ne, bn),
        in_specs=[pl.BlockSpec((1, 1, c, d), lambda e, b: (b, e, 0, 0)),
                  pl.BlockSpec((1, d, f), lambda e, b: (e, 0, 0)),
                  pl.BlockSpec((1, d, f), lambda e, b: (e, 0, 0)),
                  pl.BlockSpec((1, f, d), lambda e, b: (e, 0, 0))],
        out_specs=pl.BlockSpec((1, 1, c, d), lambda e, b: (b, e, 0, 0)),
        out_shape=jax.ShapeDtypeStruct((bn, ne, c, d), BF16),
        compiler_params=_params("arbitrary", "arbitrary"),
        name="ffn",
    )(xe, wg, wu, wd)


def _combine_kernel(start_ref, ovf_ref, slot_ref, aff_ref, y_ref, x1_ref, g2_ref, gf_ref, o_ref):
    b = pl.program_id(0)
    j = pl.program_id(1)
    n_tiles = pl.num_programs(1)
    tile = slot_ref.shape[1]
    slot = slot_ref[0].astype(F32)
    aff = aff_ref[0]

    def finish(moe):
        xo = x1_ref[0] + g2_ref[0] * moe
        o_ref[0] = (xo * lax.rsqrt(jnp.mean(xo * xo, axis=-1, keepdims=True) + EPS)) * gf_ref[...]

    def weighted_onehot(e, width, first):
        col = lax.broadcasted_iota(jnp.int32, (tile, width), 1).astype(F32)
        return jnp.where(col == slot[:, e:e + 1] - first, aff[:, e:e + 1], 0.0).astype(BF16)

    overflow = ovf_ref[b * n_tiles + j] != 0

    @pl.when(jnp.logical_not(overflow))
    def _():
        lhs, rhs = [], []
        for e in range(N_EXPERTS):
            start = pl.multiple_of(start_ref[(b * n_tiles + j) * N_EXPERTS + e], BF16_SUBLANES)
            lhs.append(weighted_onehot(e, WIN_SLOTS, start.astype(F32)))
            rhs.append(y_ref[0, e, pl.ds(start, WIN_SLOTS), :])
        finish(jnp.dot(jnp.concatenate(lhs, axis=1), jnp.concatenate(rhs, axis=0),
                       preferred_element_type=F32))

    @pl.when(overflow)
    def _():
        moe = jnp.zeros(x1_ref.shape[1:], F32)
        for e in range(N_EXPERTS):
            moe = moe + jnp.dot(weighted_onehot(e, CAPACITY, 0.0), y_ref[0, e], preferred_element_type=F32)
        finish(moe)


def _combine(start, overflow, slot_t, aff, y, x1, g2, gf):
    bn, n, d = x1.shape
    tile = WIN_TOK_TILE
    tok = lambda b, i, *_: (b, i, 0)
    return pl.pallas_call(
        _combine_kernel,
        grid_spec=pltpu.PrefetchScalarGridSpec(
            num_scalar_prefetch=2,
            grid=(bn, n // tile),
            in_specs=[pl.BlockSpec((1, tile, N_EXPERTS), tok),
                      pl.BlockSpec((1, tile, N_EXPERTS), tok),
                      pl.BlockSpec((1, N_EXPERTS, CAPACITY, d), lambda b, i, *_: (b, 0, 0, 0)),
                      pl.BlockSpec((1, tile, d), tok),
                      pl.BlockSpec((1, 1, d), lambda b, i, *_: (b, 0, 0)),
                      pl.BlockSpec((1, d), lambda b, i, *_: (0, 0))],
            out_specs=pl.BlockSpec((1, tile, d), tok)),
        out_shape=jax.ShapeDtypeStruct((bn, n, d), F32),
        compiler_params=_params("arbitrary", "arbitrary"),
        name="combine",
    )(start, overflow, slot_t, aff, y, x1, g2, gf)


def _rope_tables(L):
    pos = jnp.arange(L)
    row = (pos // GRID_W).astype(F32)
    col = (pos % GRID_W).astype(F32)
    half = HEAD_DIM // 2
    freqs = ROPE_BASE ** (-jnp.arange(0, half, 2, dtype=F32) / half)
    ang_r = row[:, None] * freqs[None, :]
    ang_c = col[:, None] * freqs[None, :]
    cos_h = jnp.concatenate([jnp.cos(ang_r), jnp.cos(ang_r), jnp.cos(ang_c), jnp.cos(ang_c)], axis=1)
    sin_h = jnp.concatenate([-jnp.sin(ang_r), jnp.sin(ang_r), -jnp.sin(ang_c), jnp.sin(ang_c)], axis=1)
    return jnp.tile(cos_h, (1, LANES // HEAD_DIM)), jnp.tile(sin_h, (1, LANES // HEAD_DIM))


def kernel(x, c, ctx, c_ctx, w_ada, b_ada, norm1_g, w_in, b_merge_gate, gmlp_ln_g, gmlp_ln_b, w_spatial,
           b_spatial, attn_sink, w_proj_a, w_proj_b, w_out, norm2_g, w_router, w_exp_gate, w_exp_up,
           w_exp_down, final_g):
    bn, L, d = x.shape
    depth = w_ada.shape[0]
    assert depth == 1, "the combine kernel fuses the final rmsnorm, so it must follow the only layer"
    cos_t, sin_t = _rope_tables(L)
    ada_rows = 16
    cc = jnp.zeros((ada_rows, d), F32).at[:bn].set(c).at[bn].set(c_ctx)

    for l in range(depth):
        mod = _ada(cc, w_ada[l], b_ada[l][None, :])
        sh1, sc1, g1, sh2, sc2, g2 = [mod[:bn, None, i * d:(i + 1) * d] for i in range(6)]
        csh1 = mod[bn:bn + 1, 0:d]
        csc1 = mod[bn:bn + 1, d:2 * d]
        w_in_b = w_in[l].astype(BF16)
        g1n = norm1_g[l][None, :]

        ya, q, kd, vd, ga, gb = _inproj(
            x, sh1, sc1, g1n, w_in_b, b_merge_gate[l][None, :], gmlp_ln_g[l][None, :],
            gmlp_ln_b[l][None, :], w_spatial[l].astype(BF16), b_spatial[l].T, cos_t, sin_t)
        kxd, vxd = _ctxkv(ctx, csh1, csc1, g1n, w_in_b[:, OFF_K:OFF_GATE])
        yb = _attn(attn_sink[l], q, kd, vd, kxd, vxd)

        x1, h2, aff = _merge(x, ya, yb, ga, gb, w_proj_a[l].astype(BF16), w_proj_b[l].astype(BF16),
                             w_out[l].astype(BF16), g1, sh2, sc2, norm2_g[l][None, :],
                             w_router[l].astype(BF16))
        slot, off = _select(jnp.swapaxes(aff, 1, 2))
        win_start, win_overflow = _window_plan(off)
        xe = _gather(slot[:, :, None, :], h2)
        y = _ffn(xe, w_exp_gate[l].astype(BF16), w_exp_up[l].astype(BF16), w_exp_down[l].astype(BF16))
        x = _combine(win_start, win_overflow, jnp.swapaxes(slot, 1, 2), aff, y, x1, g2, final_g[None, :])
    return x
```

```python
import functools

import jax
import jax.numpy as jnp
from jax import lax
from jax.experimental import pallas as pl
from jax.experimental.pallas import tpu as pltpu

F32 = jnp.float32
BF16 = jnp.bfloat16

D_MODEL = 1024
SEQ = 4096
CTX_LEN = 256
GRID_W = 64
EPS = 1e-6
CHUNK = 128
A_GROUPS = 8
A_WIDTH = 1024
N_HEADS = 16
N_KV_HEADS = 4
HEAD_DIM = 64
B_WIDTH = N_HEADS * HEAD_DIM
KV_WIDTH = N_KV_HEADS * HEAD_DIM
WINDOW = 128
BLOCK = 128
ROPE_BASE = 10000.0
N_EXPERTS = 16
CAPACITY = 2 * SEQ // N_EXPERTS
D_EXPERT = 2048
OFF_U = 0
OFF_VA = OFF_U + A_WIDTH
OFF_Q = OFF_VA + A_WIDTH
OFF_K = OFF_Q + B_WIDTH
OFF_VB = OFF_K + KV_WIDTH
OFF_GATE = OFF_VB + KV_WIDTH
IN_COLS = OFF_GATE + 2 * D_MODEL

LANES = 128
KV_DUP_WIDTH = N_KV_HEADS * LANES
VMEM_LIMIT = 56 * 1024 * 1024
MASK_VALUE = -1e30

TOK_TILE = 512
WIN_TOK_TILE = 512
WIN_SLOTS = 128
BF16_SUBLANES = 16
CUMSUM_BLOCK = 512


def _params(*sem):
    return pltpu.CompilerParams(dimension_semantics=sem, vmem_limit_bytes=VMEM_LIMIT)


def _gelu_tanh(x):
    return 0.5 * x * (1.0 + jnp.tanh(0.7978845608028654 * (x + 0.044715 * (x * x * x))))


def _rms_modulate(x, gain, shift, scale):
    y = x * lax.rsqrt(jnp.mean(x * x, axis=-1, keepdims=True) + EPS)
    return (y * gain) * (1.0 + scale) + shift


def _dup_kv_heads(x):
    lane = lax.broadcasted_iota(jnp.int32, x.shape, 1)
    lo = lane < HEAD_DIM
    r = pltpu.roll(x, HEAD_DIM, axis=1)
    return jnp.where(lo, x, r), jnp.where(lo, r, x)


def _ada_kernel(c_ref, w_ref, b_ref, o_ref):
    c = c_ref[...]
    s = c * jax.nn.sigmoid(c)
    o_ref[...] = jnp.dot(s, w_ref[...], preferred_element_type=F32,
                         precision=lax.Precision.HIGHEST) + b_ref[...]


def _ada(cc, w, b):
    rows, d = cc.shape
    cols = w.shape[1]
    tn = 1536
    return pl.pallas_call(
        _ada_kernel,
        grid=(cols // tn,),
        in_specs=[pl.BlockSpec((rows, d), lambda j: (0, 0)),
                  pl.BlockSpec((d, tn), lambda j: (0, j)),
                  pl.BlockSpec((1, tn), lambda j: (0, j))],
        out_specs=pl.BlockSpec((rows, tn), lambda j: (0, j)),
        out_shape=jax.ShapeDtypeStruct((rows, cols), F32),
        compiler_params=_params("arbitrary"),
        name="ada",
    )(cc, w, b)


def _inproj_kernel(x_ref, sh_ref, sc_ref, g_ref, w_ref, bg_ref, lng_ref, lnb_ref, ws_ref, bs_ref,
                   cos_ref, sin_ref, ya_ref, q_ref, kd_ref, vd_ref, ga_ref, gb_ref):
    tm = x_ref.shape[1]
    h = _rms_modulate(x_ref[0], g_ref[...], sh_ref[0], sc_ref[0]).astype(BF16)

    def proj(lo, hi):
        return jnp.dot(h, w_ref[:, lo:hi], preferred_element_type=F32)

    u = _gelu_tanh(proj(OFF_U, OFF_VA))
    v = _gelu_tanh(proj(OFF_VA, OFF_Q))
    mu = jnp.mean(v, axis=-1, keepdims=True)
    vc = v - mu
    var = jnp.mean(vc * vc, axis=-1, keepdims=True)
    vln = ((vc * lax.rsqrt(var + EPS)) * lng_ref[...] + lnb_ref[...]).astype(BF16)
    nc = tm // CHUNK
    gd = A_WIDTH // A_GROUPS
    for g in range(A_GROUPS):
        rhs = jnp.concatenate(
            [vln[c * CHUNK:(c + 1) * CHUNK, g * gd:(g + 1) * gd] for c in range(nc)], axis=1)
        mixed = jnp.dot(ws_ref[g], rhs, preferred_element_type=F32) + bs_ref[:, g:g + 1]
        for c in range(nc):
            ya_ref[0, c * CHUNK:(c + 1) * CHUNK, g * gd:(g + 1) * gd] = (
                u[c * CHUNK:(c + 1) * CHUNK, g * gd:(g + 1) * gd]
                * mixed[:, c * CHUNK:(c + 1) * CHUNK]).astype(BF16)

    cos = cos_ref[...]
    sin = sin_ref[...]
    lane = lax.broadcasted_iota(jnp.int32, (tm, LANES), 1)
    first_half = (lane % 32) < 16

    def rope(t):
        partner = jnp.where(first_half, pltpu.roll(t, LANES - 16, axis=1), pltpu.roll(t, 16, axis=1))
        return t * cos + partner * sin

    q = proj(OFF_Q, OFF_K)
    for j in range(B_WIDTH // LANES):
        q_ref[0, :, j * LANES:(j + 1) * LANES] = (
            rope(q[:, j * LANES:(j + 1) * LANES]) * (HEAD_DIM ** -0.5)).astype(BF16)
    k = proj(OFF_K, OFF_VB)
    vb = proj(OFF_VB, OFF_GATE)
    for j in range(KV_WIDTH // LANES):
        ke, ko = _dup_kv_heads(rope(k[:, j * LANES:(j + 1) * LANES]))
        kd_ref[0, :, (2 * j) * LANES:(2 * j + 1) * LANES] = ke.astype(BF16)
        kd_ref[0, :, (2 * j + 1) * LANES:(2 * j + 2) * LANES] = ko.astype(BF16)
        ve, vo = _dup_kv_heads(vb[:, j * LANES:(j + 1) * LANES])
        vd_ref[0, :, (2 * j) * LANES:(2 * j + 1) * LANES] = ve.astype(BF16)
        vd_ref[0, :, (2 * j + 1) * LANES:(2 * j + 2) * LANES] = vo.astype(BF16)

    ga_ref[0] = jax.nn.sigmoid(proj(OFF_GATE, OFF_GATE + D_MODEL) + bg_ref[:, :D_MODEL]).astype(BF16)
    gb_ref[0] = jax.nn.sigmoid(proj(OFF_GATE + D_MODEL, IN_COLS) + bg_ref[:, D_MODEL:]).astype(BF16)


def _inproj(x, sh1, sc1, g1n, w_in, b_gate, ln_g, ln_b, w_s, b_s_t, cos_t, sin_t):
    bn, L, d = x.shape
    tm = TOK_TILE
    tok = lambda b, i: (b, i, 0)
    per_b = lambda b, i: (b, 0, 0)
    const2 = lambda b, i: (0, 0)
    out = lambda w: jax.ShapeDtypeStruct((bn, L, w), BF16)
    return pl.pallas_call(
        _inproj_kernel,
        grid=(bn, L // tm),
        in_specs=[pl.BlockSpec((1, tm, d), tok),
                  pl.BlockSpec((1, 1, d), per_b),
                  pl.BlockSpec((1, 1, d), per_b),
                  pl.BlockSpec((1, d), const2),
                  pl.BlockSpec((d, IN_COLS), const2),
                  pl.BlockSpec((1, 2 * d), const2),
                  pl.BlockSpec((1, A_WIDTH), const2),
                  pl.BlockSpec((1, A_WIDTH), const2),
                  pl.BlockSpec((A_GROUPS, CHUNK, CHUNK), lambda b, i: (0, 0, 0)),
                  pl.BlockSpec((CHUNK, A_GROUPS), const2),
                  pl.BlockSpec((tm, LANES), lambda b, i: (i, 0)),
                  pl.BlockSpec((tm, LANES), lambda b, i: (i, 0))],
        out_specs=[pl.BlockSpec((1, tm, A_WIDTH), tok),
                   pl.BlockSpec((1, tm, B_WIDTH), tok),
                   pl.BlockSpec((1, tm, KV_DUP_WIDTH), tok),
                   pl.BlockSpec((1, tm, KV_DUP_WIDTH), tok),
                   pl.BlockSpec((1, tm, d), tok),
                   pl.BlockSpec((1, tm, d), tok)],
        out_shape=[out(A_WIDTH), out(B_WIDTH), out(KV_DUP_WIDTH), out(KV_DUP_WIDTH), out(d), out(d)],
        compiler_params=_params("arbitrary", "arbitrary"),
        name="inproj",
    )(x, sh1, sc1, g1n, w_in, b_gate, ln_g, ln_b, w_s, b_s_t, cos_t, sin_t)


def _ctxkv_kernel(x_ref, sh_ref, sc_ref, g_ref, w_ref, kd_ref, vd_ref):
    h = _rms_modulate(x_ref[0], g_ref[...], sh_ref[...], sc_ref[...]).astype(BF16)
    kv = jnp.dot(h, w_ref[...], preferred_element_type=F32)
    for j in range(KV_WIDTH // LANES):
        ke, ko = _dup_kv_heads(kv[:, j * LANES:(j + 1) * LANES])
        kd_ref[0, :, (2 * j) * LANES:(2 * j + 1) * LANES] = ke.astype(BF16)
        kd_ref[0, :, (2 * j + 1) * LANES:(2 * j + 2) * LANES] = ko.astype(BF16)
        ve, vo = _dup_kv_heads(kv[:, KV_WIDTH + j * LANES:KV_WIDTH + (j + 1) * LANES])
        vd_ref[0, :, (2 * j) * LANES:(2 * j + 1) * LANES] = ve.astype(BF16)
        vd_ref[0, :, (2 * j + 1) * LANES:(2 * j + 2) * LANES] = vo.astype(BF16)


def _ctxkv(ctx, csh1, csc1, g1n, w_kv):
    bn, n, d = ctx.shape
    const2 = lambda b: (0, 0)
    blk = lambda b: (b, 0, 0)
    return pl.pallas_call(
        _ctxkv_kernel,
        grid=(bn,),
        in_specs=[pl.BlockSpec((1, n, d), blk),
                  pl.BlockSpec((1, d), const2),
                  pl.BlockSpec((1, d), const2),
                  pl.BlockSpec((1, d), const2),
                  pl.BlockSpec((d, 2 * KV_WIDTH), const2)],
        out_specs=[pl.BlockSpec((1, n, KV_DUP_WIDTH), blk), pl.BlockSpec((1, n, KV_DUP_WIDTH), blk)],
        out_shape=[jax.ShapeDtypeStruct((bn, n, KV_DUP_WIDTH), BF16)] * 2,
        compiler_params=_params("arbitrary"),
        name="ctxkv",
    )(ctx, csh1, csc1, g1n, w_kv)


def _attn_kernel(sink_ref, q_ref, kp_ref, kc_ref, kn_ref, vp_ref, vc_ref, vn_ref, kx_ref, vx_ref, o_ref):
    n = pl.program_id(1)
    rep = N_HEADS // N_KV_HEADS
    rows = rep * BLOCK
    n_loc = 3 * BLOCK
    cols = n_loc + CTX_LEN
    r = lax.broadcasted_iota(jnp.int32, (rows, cols), 0)
    s = lax.broadcasted_iota(jnp.int32, (rows, cols), 1)
    rel = s - BLOCK - (r % BLOCK)
    key_pos = (n - 1) * BLOCK + s
    valid = ((jnp.abs(rel) <= WINDOW) & (key_pos >= 0) & (key_pos < SEQ)) | (s >= n_loc)
    lo = lax.broadcasted_iota(jnp.int32, (BLOCK, LANES), 1) < HEAD_DIM
    rcol = lax.broadcasted_iota(jnp.int32, (rows, 1), 0)
    zero = jnp.zeros((BLOCK, LANES), BF16)

    for kh in range(N_KV_HEADS):
        ls = slice(kh * LANES, (kh + 1) * LANES)
        parts = []
        for j in (2 * kh, 2 * kh + 1):
            qc = q_ref[0, :, j * LANES:(j + 1) * LANES]
            parts += [jnp.where(lo, qc, zero), jnp.where(lo, zero, qc)]
        lhs = jnp.concatenate(parts, axis=0)
        k_all = jnp.concatenate([kp_ref[0, :, ls], kc_ref[0, :, ls], kn_ref[0, :, ls], kx_ref[0, :, ls]], axis=0)
        v_all = jnp.concatenate([vp_ref[0, :, ls], vc_ref[0, :, ls], vn_ref[0, :, ls], vx_ref[0, :, ls]], axis=0)
        logits = lax.dot_general(lhs, k_all, (((1,), (1,)), ((), ())), preferred_element_type=F32)
        logits = jnp.where(valid, logits, MASK_VALUE)
        sink = jnp.where(rcol < BLOCK, sink_ref[rep * kh],
                         jnp.where(rcol < 2 * BLOCK, sink_ref[rep * kh + 1],
                                   jnp.where(rcol < 3 * BLOCK, sink_ref[rep * kh + 2], sink_ref[rep * kh + 3])))
        m = jnp.maximum(jnp.max(logits, axis=1, keepdims=True), sink)
        p = jnp.exp(logits - m)
        denom = jnp.sum(p, axis=1, keepdims=True) + jnp.exp(sink - m)
        o = jnp.dot(p.astype(BF16), v_all, preferred_element_type=F32) / denom
        o_ref[0, :, (2 * kh) * LANES:(2 * kh + 1) * LANES] = jnp.where(
            lo, o[0:BLOCK], o[BLOCK:2 * BLOCK]).astype(BF16)
        o_ref[0, :, (2 * kh + 1) * LANES:(2 * kh + 2) * LANES] = jnp.where(
            lo, o[2 * BLOCK:3 * BLOCK], o[3 * BLOCK:4 * BLOCK]).astype(BF16)


def _attn(sink, q, kd, vd, kxd, vxd):
    bn, L, _ = q.shape
    nb = L // BLOCK
    cur = lambda b, n: (b, n, 0)
    prev = lambda b, n: (b, jnp.maximum(n - 1, 0), 0)
    nxt = lambda b, n: (b, jnp.minimum(n + 1, nb - 1), 0)
    per_b = lambda b, n: (b, 0, 0)
    kv_blk = (1, BLOCK, KV_DUP_WIDTH)
    return pl.pallas_call(
        _attn_kernel,
        grid=(bn, nb),
        in_specs=[pl.BlockSpec(memory_space=pltpu.SMEM),
                  pl.BlockSpec((1, BLOCK, B_WIDTH), cur),
                  pl.BlockSpec(kv_blk, prev), pl.BlockSpec(kv_blk, cur), pl.BlockSpec(kv_blk, nxt),
                  pl.BlockSpec(kv_blk, prev), pl.BlockSpec(kv_blk, cur), pl.BlockSpec(kv_blk, nxt),
                  pl.BlockSpec((1, CTX_LEN, KV_DUP_WIDTH), per_b),
                  pl.BlockSpec((1, CTX_LEN, KV_DUP_WIDTH), per_b)],
        out_specs=pl.BlockSpec((1, BLOCK, B_WIDTH), cur),
        out_shape=jax.ShapeDtypeStruct((bn, L, B_WIDTH), BF16),
        compiler_params=_params("arbitrary", "arbitrary"),
        name="attn",
    )(sink, q, kd, kd, kd, vd, vd, vd, kxd, vxd)


def _merge_kernel(x_ref, ya_ref, yb_ref, ga_ref, gb_ref, wa_ref, wb_ref, wo_ref, g1_ref, sh_ref, sc_ref,
                  gn_ref, wr_ref, x1_ref, h2_ref, aff_ref):
    pa = jnp.dot(ya_ref[0], wa_ref[...], preferred_element_type=F32)
    pb = jnp.dot(yb_ref[0], wb_ref[...], preferred_element_type=F32)
    merged = (ga_ref[0].astype(F32) * pa + gb_ref[0].astype(F32) * pb).astype(BF16)
    x1 = x_ref[0] + g1_ref[0] * jnp.dot(merged, wo_ref[...], preferred_element_type=F32)
    x1_ref[0] = x1
    h2 = _rms_modulate(x1, gn_ref[...], sh_ref[0], sc_ref[0]).astype(BF16)
    h2_ref[0] = h2
    logits = jnp.dot(h2, wr_ref[...], preferred_element_type=F32)
    e = jnp.exp(logits - jnp.max(logits, axis=-1, keepdims=True))
    aff_ref[0] = e / jnp.sum(e, axis=-1, keepdims=True)


def _merge(x, ya, yb, ga, gb, wa, wb, wo, g1, sh2, sc2, g2n, wr):
    bn, L, d = x.shape
    tm = TOK_TILE
    tok = lambda b, i: (b, i, 0)
    per_b = lambda b, i: (b, 0, 0)
    const2 = lambda b, i: (0, 0)
    return pl.pallas_call(
        _merge_kernel,
        grid=(bn, L // tm),
        in_specs=[pl.BlockSpec((1, tm, d), tok)] * 5
        + [pl.BlockSpec((d, d), const2)] * 3
        + [pl.BlockSpec((1, 1, d), per_b)] * 3
        + [pl.BlockSpec((1, d), const2), pl.BlockSpec((d, N_EXPERTS), const2)],
        out_specs=[pl.BlockSpec((1, tm, d), tok), pl.BlockSpec((1, tm, d), tok),
                   pl.BlockSpec((1, tm, N_EXPERTS), tok)],
        out_shape=[jax.ShapeDtypeStruct((bn, L, d), F32), jax.ShapeDtypeStruct((bn, L, d), BF16),
                   jax.ShapeDtypeStruct((bn, L, N_EXPERTS), F32)],
        compiler_params=_params("arbitrary", "arbitrary"),
        name="merge",
    )(x, ya, yb, ga, gb, wa, wb, wo, g1, sh2, sc2, g2n, wr)


def _excl_prefix_count(m, upper):
    n = m.shape[1]
    outs = []
    carry = jnp.zeros((m.shape[0], 1), F32)
    for blk in range(n // CUMSUM_BLOCK):
        mb = m[:, blk * CUMSUM_BLOCK:(blk + 1) * CUMSUM_BLOCK]
        outs.append(jnp.dot(mb, upper, preferred_element_type=F32) + carry)
        carry = carry + jnp.sum(mb.astype(F32), axis=1, keepdims=True)
    return jnp.concatenate(outs, axis=1)


def _select_kernel(aff_ref, slot_ref, off_ref):
    x = aff_ref[0]
    one = jnp.ones(x.shape, F32)
    zero = jnp.zeros(x.shape, F32)

    def count(mask):
        return jnp.sum(jnp.where(mask, one, zero), axis=1, keepdims=True)

    def body(i, t):
        cand = t | jnp.left_shift(jnp.int32(1), 30 - i)
        return jnp.where(count(x >= lax.bitcast_convert_type(cand, F32)) >= CAPACITY, cand, t)

    thr = lax.bitcast_convert_type(
        lax.fori_loop(0, 31, body, jnp.zeros((x.shape[0], 1), jnp.int32)), F32)
    gt = x > thr
    eq = x == thr
    need = CAPACITY - count(gt)
    ri = lax.broadcasted_iota(jnp.int32, (CUMSUM_BLOCK, CUMSUM_BLOCK), 0)
    ci = lax.broadcasted_iota(jnp.int32, (CUMSUM_BLOCK, CUMSUM_BLOCK), 1)
    upper = jnp.where(ri < ci, 1.0, 0.0).astype(BF16)
    eq_rank = _excl_prefix_count(jnp.where(eq, one, zero).astype(BF16), upper)
    sel = gt | (eq & (eq_rank < need))
    slot = _excl_prefix_count(jnp.where(sel, one, zero).astype(BF16), upper)
    slot_ref[0] = jnp.where(sel, slot, -1.0).astype(jnp.int32)
    lane = lax.broadcasted_iota(jnp.int32, (x.shape[0], LANES), 1)
    off = jnp.zeros((x.shape[0], LANES), F32)
    for j in range(x.shape[1] // WIN_TOK_TILE):
        off = jnp.where(lane == j, slot[:, j * WIN_TOK_TILE:j * WIN_TOK_TILE + 1], off)
    off_ref[0] = off.astype(jnp.int32)


def _select(aff_t):
    bn, e, n = aff_t.shape
    blk = lambda b: (b, 0, 0)
    return pl.pallas_call(
        _select_kernel,
        grid=(bn,),
        in_specs=[pl.BlockSpec((1, e, n), blk)],
        out_specs=[pl.BlockSpec((1, e, n), blk), pl.BlockSpec((1, e, LANES), blk)],
        out_shape=[jax.ShapeDtypeStruct((bn, e, n), jnp.int32),
                   jax.ShapeDtypeStruct((bn, e, LANES), jnp.int32)],
        compiler_params=_params("arbitrary"),
        name="select",
    )(aff_t)


def _window_plan(off):
    n_tiles = SEQ // WIN_TOK_TILE
    first = off[:, :, :n_tiles]
    end = jnp.concatenate([first[:, :, 1:], jnp.full_like(first[:, :, :1], CAPACITY)], axis=2)
    start = jnp.minimum((first // BF16_SUBLANES) * BF16_SUBLANES, CAPACITY - WIN_SLOTS)
    overflow = jnp.any(end - start > WIN_SLOTS, axis=1).astype(jnp.int32)
    return jnp.swapaxes(start, 1, 2).reshape(-1), overflow.reshape(-1)


GATHER_GROUP = 4


def _gather_kernel(start_ref, ovf_ref, slot_ref, h_ref, xe_ref):
    b = pl.program_id(0)
    j = pl.program_id(1)
    n_tiles = pl.num_programs(1)
    tile = h_ref.shape[1]

    @pl.when(j == 0)
    def _():
        xe_ref[...] = jnp.zeros_like(xe_ref)

    h = h_ref[0]
    slot = slot_ref[0].astype(F32)

    def onehot(e, height, first):
        row = lax.broadcasted_iota(jnp.int32, (height, tile), 0).astype(F32)
        return jnp.where(row == slot[e:e + 1, :] - first, 1.0, 0.0).astype(BF16)

    def accumulate(e, rows, part):
        xe_ref[0, e, rows, :] = (xe_ref[0, e, rows, :].astype(F32) + part).astype(BF16)

    overflow = ovf_ref[b * n_tiles + j] != 0

    @pl.when(jnp.logical_not(overflow))
    def _():
        for e0 in range(0, N_EXPERTS, GATHER_GROUP):
            starts = [pl.multiple_of(start_ref[(b * n_tiles + j) * N_EXPERTS + e], BF16_SUBLANES)
                      for e in range(e0, e0 + GATHER_GROUP)]
            lhs = jnp.concatenate([onehot(e0 + i, WIN_SLOTS, s.astype(F32)) for i, s in enumerate(starts)],
                                  axis=0)
            part = jnp.dot(lhs, h, preferred_element_type=F32)
            for i, s in enumerate(starts):
                accumulate(e0 + i, pl.ds(s, WIN_SLOTS), part[i * WIN_SLOTS:(i + 1) * WIN_SLOTS])

    @pl.when(overflow)
    def _():
        for e in range(N_EXPERTS):
            accumulate(e, slice(None), jnp.dot(onehot(e, CAPACITY, 0.0), h, preferred_element_type=F32))


def _gather(start, overflow, slot, h2):
    bn, n, d = h2.shape
    tile = WIN_TOK_TILE
    return pl.pallas_call(
        _gather_kernel,
        grid_spec=pltpu.PrefetchScalarGridSpec(
            num_scalar_prefetch=2,
            grid=(bn, n // tile),
            in_specs=[pl.BlockSpec((1, N_EXPERTS, tile), lambda b, i, *_: (b, 0, i)),
                      pl.BlockSpec((1, tile, d), lambda b, i, *_: (b, i, 0))],
            out_specs=pl.BlockSpec((1, N_EXPERTS, CAPACITY, d), lambda b, i, *_: (b, 0, 0, 0))),
        out_shape=jax.ShapeDtypeStruct((bn, N_EXPERTS, CAPACITY, d), BF16),
        compiler_params=_params("arbitrary", "arbitrary"),
        name="gather",
    )(start, overflow, slot, h2)


def _ffn_kernel(xe_ref, wg_ref, wu_ref, wd_ref, y_ref):
    xe = xe_ref[0, 0]
    fc = 512
    acc = jnp.zeros((xe.shape[0], wd_ref.shape[2]), F32)
    for j in range(D_EXPERT // fc):
        g = jnp.dot(xe, wg_ref[0, :, j * fc:(j + 1) * fc], preferred_element_type=F32)
        u = jnp.dot(xe, wu_ref[0, :, j * fc:(j + 1) * fc], preferred_element_type=F32)
        a = ((g * jax.nn.sigmoid(g)) * u).astype(BF16)
        acc = acc + jnp.dot(a, wd_ref[0, j * fc:(j + 1) * fc, :], preferred_element_type=F32)
    y_ref[0, 0] = acc.astype(BF16)


def _ffn(xe, wg, wu, wd):
    bn, ne, c, d = xe.shape
    f = wg.shape[2]
    return pl.pallas_call(
        _ffn_kernel,
        grid=(ne, bn),
        in_specs=[pl.BlockSpec((1, 1, c, d), lambda e, b: (b, e, 0, 0)),
                  pl.BlockSpec((1, d, f), lambda e, b: (e, 0, 0)),
                  pl.BlockSpec((1, d, f), lambda e, b: (e, 0, 0)),
                  pl.BlockSpec((1, f, d), lambda e, b: (e, 0, 0))],
        out_specs=pl.BlockSpec((1, 1, c, d), lambda e, b: (b, e, 0, 0)),
        out_shape=jax.ShapeDtypeStruct((bn, ne, c, d), BF16),
        compiler_params=_params("arbitrary", "arbitrary"),
        name="ffn",
    )(xe, wg, wu, wd)


def _combine_kernel(start_ref, ovf_ref, slot_ref, aff_ref, y_ref, x1_ref, g2_ref, gf_ref, o_ref):
    b = pl.program_id(0)
    j = pl.program_id(1)
    n_tiles = pl.num_programs(1)
    tile = slot_ref.shape[1]
    slot = slot_ref[0].astype(F32)
    aff = aff_ref[0]

    def finish(moe):
        xo = x1_ref[0] + g2_ref[0] * moe
        o_ref[0] = (xo * lax.rsqrt(jnp.mean(xo * xo, axis=-1, keepdims=True) + EPS)) * gf_ref[...]

    def weighted_onehot(e, width, first):
        col = lax.broadcasted_iota(jnp.int32, (tile, width), 1).astype(F32)
        return jnp.where(col == slot[:, e:e + 1] - first, aff[:, e:e + 1], 0.0).astype(BF16)

    overflow = ovf_ref[b * n_tiles + j] != 0

    @pl.when(jnp.logical_not(overflow))
    def _():
        lhs, rhs = [], []
        for e in range(N_EXPERTS):
            start = pl.multiple_of(start_ref[(b * n_tiles + j) * N_EXPERTS + e], BF16_SUBLANES)
            lhs.append(weighted_onehot(e, WIN_SLOTS, start.astype(F32)))
            rhs.append(y_ref[0, e, pl.ds(start, WIN_SLOTS), :])
        finish(jnp.dot(jnp.concatenate(lhs, axis=1), jnp.concatenate(rhs, axis=0),
                       preferred_element_type=F32))

    @pl.when(overflow)
    def _():
        moe = jnp.zeros(x1_ref.shape[1:], F32)
        for e in range(N_EXPERTS):
            moe = moe + jnp.dot(weighted_onehot(e, CAPACITY, 0.0), y_ref[0, e], preferred_element_type=F32)
        finish(moe)


def _combine(start, overflow, slot_t, aff, y, x1, g2, gf):
    bn, n, d = x1.shape
    tile = WIN_TOK_TILE
    tok = lambda b, i, *_: (b, i, 0)
    return pl.pallas_call(
        _combine_kernel,
        grid_spec=pltpu.PrefetchScalarGridSpec(
            num_scalar_prefetch=2,
            grid=(bn, n // tile),
            in_specs=[pl.BlockSpec((1, tile, N_EXPERTS), tok),
                      pl.BlockSpec((1, tile, N_EXPERTS), tok),
                      pl.BlockSpec((1, N_EXPERTS, CAPACITY, d), lambda b, i, *_: (b, 0, 0, 0)),
                      pl.BlockSpec((1, tile, d), tok),
                      pl.BlockSpec((1, 1, d), lambda b, i, *_: (b, 0, 0)),
                      pl.BlockSpec((1, d), lambda b, i, *_: (0, 0))],
            out_specs=pl.BlockSpec((1, tile, d), tok)),
        out_shape=jax.ShapeDtypeStruct((bn, n, d), F32),
        compiler_params=_params("arbitrary", "arbitrary"),
        name="combine",
    )(start, overflow, slot_t, aff, y, x1, g2, gf)


def _rope_tables(L):
    pos = jnp.arange(L)
    row = (pos // GRID_W).astype(F32)
    col = (pos % GRID_W).astype(F32)
    half = HEAD_DIM // 2
    freqs = ROPE_BASE ** (-jnp.arange(0, half, 2, dtype=F32) / half)
    ang_r = row[:, None] * freqs[None, :]
    ang_c = col[:, None] * freqs[None, :]
    cos_h = jnp.concatenate([jnp.cos(ang_r), jnp.cos(ang_r), jnp.cos(ang_c), jnp.cos(ang_c)], axis=1)
    sin_h = jnp.concatenate([-jnp.sin(ang_r), jnp.sin(ang_r), -jnp.sin(ang_c), jnp.sin(ang_c)], axis=1)
    return jnp.tile(cos_h, (1, LANES // HEAD_DIM)), jnp.tile(sin_h, (1, LANES // HEAD_DIM))


def kernel(x, c, ctx, c_ctx, w_ada, b_ada, norm1_g, w_in, b_merge_gate, gmlp_ln_g, gmlp_ln_b, w_spatial,
           b_spatial, attn_sink, w_proj_a, w_proj_b, w_out, norm2_g, w_router, w_exp_gate, w_exp_up,
           w_exp_down, final_g):
    bn, L, d = x.shape
    depth = w_ada.shape[0]
    assert depth == 1, "the combine kernel fuses the final rmsnorm, so it must follow the only layer"
    cos_t, sin_t = _rope_tables(L)
    ada_rows = 16
    cc = jnp.zeros((ada_rows, d), F32).at[:bn].set(c).at[bn].set(c_ctx)

    for l in range(depth):
        mod = _ada(cc, w_ada[l], b_ada[l][None, :])
        sh1, sc1, g1, sh2, sc2, g2 = [mod[:bn, None, i * d:(i + 1) * d] for i in range(6)]
        csh1 = mod[bn:bn + 1, 0:d]
        csc1 = mod[bn:bn + 1, d:2 * d]
        w_in_b = w_in[l].astype(BF16)
        g1n = norm1_g[l][None, :]

        ya, q, kd, vd, ga, gb = _inproj(
            x, sh1, sc1, g1n, w_in_b, b_merge_gate[l][None, :], gmlp_ln_g[l][None, :],
            gmlp_ln_b[l][None, :], w_spatial[l].astype(BF16), b_spatial[l].T, cos_t, sin_t)
        kxd, vxd = _ctxkv(ctx, csh1, csc1, g1n, w_in_b[:, OFF_K:OFF_GATE])
        yb = _attn(attn_sink[l], q, kd, vd, kxd, vxd)

        x1, h2, aff = _merge(x, ya, yb, ga, gb, w_proj_a[l].astype(BF16), w_proj_b[l].astype(BF16),
                             w_out[l].astype(BF16), g1, sh2, sc2, norm2_g[l][None, :],
                             w_router[l].astype(BF16))
        slot, off = _select(jnp.swapaxes(aff, 1, 2))
        win_start, win_overflow = _window_plan(off)
        xe = _gather(win_start, win_overflow, slot, h2)
        y = _ffn(xe, w_exp_gate[l].astype(BF16), w_exp_up[l].astype(BF16), w_exp_down[l].astype(BF16))
        x = _combine(win_start, win_overflow, jnp.swapaxes(slot, 1, 2), aff, y, x1, g2, final_g[None, :])
    return x
```

```python
import functools

import jax
import jax.numpy as jnp
from jax import lax
from jax.experimental import pallas as pl
from jax.experimental.pallas import tpu as pltpu

F32 = jnp.float32
BF16 = jnp.bfloat16

D_MODEL = 1024
SEQ = 4096
CTX_LEN = 256
GRID_W = 64
EPS = 1e-6
CHUNK = 128
A_GROUPS = 8
A_WIDTH = 1024
N_HEADS = 16
N_KV_HEADS = 4
HEAD_DIM = 64
B_WIDTH = N_HEADS * HEAD_DIM
KV_WIDTH = N_KV_HEADS * HEAD_DIM
WINDOW = 128
BLOCK = 128
ROPE_BASE = 10000.0
N_EXPERTS = 16
CAPACITY = 2 * SEQ // N_EXPERTS
D_EXPERT = 2048
OFF_U = 0
OFF_VA = OFF_U + A_WIDTH
OFF_Q = OFF_VA + A_WIDTH
OFF_K = OFF_Q + B_WIDTH
OFF_VB = OFF_K + KV_WIDTH
OFF_GATE = OFF_VB + KV_WIDTH
IN_COLS = OFF_GATE + 2 * D_MODEL

LANES = 128
KV_DUP_WIDTH = N_KV_HEADS * LANES
VMEM_LIMIT = 56 * 1024 * 1024
MASK_VALUE = -1e30
LOG2E = 1.4426950408889634

TOK_TILE = 512
WIN_TOK_TILE = 512
WIN_SLOTS = 128
BF16_SUBLANES = 16
CUMSUM_BLOCK = 512


def _params(*sem):
    return pltpu.CompilerParams(dimension_semantics=sem, vmem_limit_bytes=VMEM_LIMIT)


def _gelu_tanh(x):
    return 0.5 * x * (1.0 + jnp.tanh(0.7978845608028654 * (x + 0.044715 * (x * x * x))))


def _rms_modulate(x, gain, shift, scale):
    y = x * lax.rsqrt(jnp.mean(x * x, axis=-1, keepdims=True) + EPS)
    return (y * gain) * (1.0 + scale) + shift


def _dup_kv_heads(x):
    lane = lax.broadcasted_iota(jnp.int32, x.shape, 1)
    lo = lane < HEAD_DIM
    r = pltpu.roll(x, HEAD_DIM, axis=1)
    return jnp.where(lo, x, r), jnp.where(lo, r, x)


def _kv_heads_with_ones(x):
    lane = lax.broadcasted_iota(jnp.int32, x.shape, 1)
    lo = lane < HEAD_DIM
    return jnp.where(lo, x, 1.0), jnp.where(lo, pltpu.roll(x, HEAD_DIM, axis=1), 1.0)


def _ada_kernel(c_ref, w_ref, b_ref, o_ref):
    c = c_ref[...]
    s = c * jax.nn.sigmoid(c)
    o_ref[...] = jnp.dot(s, w_ref[...], preferred_element_type=F32,
                         precision=lax.Precision.HIGHEST) + b_ref[...]


def _ada(cc, w, b):
    rows, d = cc.shape
    cols = w.shape[1]
    tn = 1536
    return pl.pallas_call(
        _ada_kernel,
        grid=(cols // tn,),
        in_specs=[pl.BlockSpec((rows, d), lambda j: (0, 0)),
                  pl.BlockSpec((d, tn), lambda j: (0, j)),
                  pl.BlockSpec((1, tn), lambda j: (0, j))],
        out_specs=pl.BlockSpec((rows, tn), lambda j: (0, j)),
        out_shape=jax.ShapeDtypeStruct((rows, cols), F32),
        compiler_params=_params("arbitrary"),
        name="ada",
    )(cc, w, b)


def _inproj_kernel(x_ref, sh_ref, sc_ref, g_ref, w_ref, bg_ref, lng_ref, lnb_ref, ws_ref, bs_ref,
                   cos_ref, sin_ref, ya_ref, q_ref, kd_ref, vd_ref, ga_ref, gb_ref):
    tm = x_ref.shape[1]
    h = _rms_modulate(x_ref[0], g_ref[...], sh_ref[0], sc_ref[0]).astype(BF16)

    def proj(lo, hi):
        return jnp.dot(h, w_ref[:, lo:hi], preferred_element_type=F32)

    u = _gelu_tanh(proj(OFF_U, OFF_VA))
    v = _gelu_tanh(proj(OFF_VA, OFF_Q))
    mu = jnp.mean(v, axis=-1, keepdims=True)
    vc = v - mu
    var = jnp.mean(vc * vc, axis=-1, keepdims=True)
    vln = ((vc * lax.rsqrt(var + EPS)) * lng_ref[...] + lnb_ref[...]).astype(BF16)
    nc = tm // CHUNK
    gd = A_WIDTH // A_GROUPS
    for g in range(A_GROUPS):
        rhs = jnp.concatenate(
            [vln[c * CHUNK:(c + 1) * CHUNK, g * gd:(g + 1) * gd] for c in range(nc)], axis=1)
        mixed = jnp.dot(ws_ref[g], rhs, preferred_element_type=F32) + bs_ref[:, g:g + 1]
        for c in range(nc):
            ya_ref[0, c * CHUNK:(c + 1) * CHUNK, g * gd:(g + 1) * gd] = (
                u[c * CHUNK:(c + 1) * CHUNK, g * gd:(g + 1) * gd]
                * mixed[:, c * CHUNK:(c + 1) * CHUNK]).astype(BF16)

    cos = cos_ref[...]
    sin = sin_ref[...]
    lane = lax.broadcasted_iota(jnp.int32, (tm, LANES), 1)
    first_half = (lane % 32) < 16

    def rope(t):
        partner = jnp.where(first_half, pltpu.roll(t, LANES - 16, axis=1), pltpu.roll(t, 16, axis=1))
        return t * cos + partner * sin

    q = proj(OFF_Q, OFF_K)
    for j in range(B_WIDTH // LANES):
        q_ref[0, :, j * LANES:(j + 1) * LANES] = (
            rope(q[:, j * LANES:(j + 1) * LANES]) * (HEAD_DIM ** -0.5 * LOG2E)).astype(BF16)
    k = proj(OFF_K, OFF_VB)
    vb = proj(OFF_VB, OFF_GATE)
    for j in range(KV_WIDTH // LANES):
        ke, ko = _dup_kv_heads(rope(k[:, j * LANES:(j + 1) * LANES]))
        kd_ref[0, :, (2 * j) * LANES:(2 * j + 1) * LANES] = ke.astype(BF16)
        kd_ref[0, :, (2 * j + 1) * LANES:(2 * j + 2) * LANES] = ko.astype(BF16)
        ve, vo = _kv_heads_with_ones(vb[:, j * LANES:(j + 1) * LANES])
        vd_ref[0, :, (2 * j) * LANES:(2 * j + 1) * LANES] = ve.astype(BF16)
        vd_ref[0, :, (2 * j + 1) * LANES:(2 * j + 2) * LANES] = vo.astype(BF16)

    ga_ref[0] = jax.nn.sigmoid(proj(OFF_GATE, OFF_GATE + D_MODEL) + bg_ref[:, :D_MODEL]).astype(BF16)
    gb_ref[0] = jax.nn.sigmoid(proj(OFF_GATE + D_MODEL, IN_COLS) + bg_ref[:, D_MODEL:]).astype(BF16)


def _inproj(x, sh1, sc1, g1n, w_in, b_gate, ln_g, ln_b, w_s, b_s_t, cos_t, sin_t):
    bn, L, d = x.shape
    tm = TOK_TILE
    tok = lambda b, i: (b, i, 0)
    per_b = lambda b, i: (b, 0, 0)
    const2 = lambda b, i: (0, 0)
    out = lambda w: jax.ShapeDtypeStruct((bn, L, w), BF16)
    return pl.pallas_call(
        _inproj_kernel,
        grid=(bn, L // tm),
        in_specs=[pl.BlockSpec((1, tm, d), tok),
                  pl.BlockSpec((1, 1, d), per_b),
                  pl.BlockSpec((1, 1, d), per_b),
                  pl.BlockSpec((1, d), const2),
                  pl.BlockSpec((d, IN_COLS), const2),
                  pl.BlockSpec((1, 2 * d), const2),
                  pl.BlockSpec((1, A_WIDTH), const2),
                  pl.BlockSpec((1, A_WIDTH), const2),
                  pl.BlockSpec((A_GROUPS, CHUNK, CHUNK), lambda b, i: (0, 0, 0)),
                  pl.BlockSpec((CHUNK, A_GROUPS), const2),
                  pl.BlockSpec((tm, LANES), lambda b, i: (i, 0)),
                  pl.BlockSpec((tm, LANES), lambda b, i: (i, 0))],
        out_specs=[pl.BlockSpec((1, tm, A_WIDTH), tok),
                   pl.BlockSpec((1, tm, B_WIDTH), tok),
                   pl.BlockSpec((1, tm, KV_DUP_WIDTH), tok),
                   pl.BlockSpec((1, tm, KV_DUP_WIDTH), tok),
                   pl.BlockSpec((1, tm, d), tok),
                   pl.BlockSpec((1, tm, d), tok)],
        out_shape=[out(A_WIDTH), out(B_WIDTH), out(KV_DUP_WIDTH), out(KV_DUP_WIDTH), out(d), out(d)],
        compiler_params=_params("arbitrary", "arbitrary"),
        name="inproj",
    )(x, sh1, sc1, g1n, w_in, b_gate, ln_g, ln_b, w_s, b_s_t, cos_t, sin_t)


def _ctxkv_kernel(x_ref, sh_ref, sc_ref, g_ref, w_ref, kd_ref, vd_ref):
    h = _rms_modulate(x_ref[0], g_ref[...], sh_ref[...], sc_ref[...]).astype(BF16)
    kv = jnp.dot(h, w_ref[...], preferred_element_type=F32)
    for j in range(KV_WIDTH // LANES):
        ke, ko = _dup_kv_heads(kv[:, j * LANES:(j + 1) * LANES])
        kd_ref[0, :, (2 * j) * LANES:(2 * j + 1) * LANES] = ke.astype(BF16)
        kd_ref[0, :, (2 * j + 1) * LANES:(2 * j + 2) * LANES] = ko.astype(BF16)
        ve, vo = _kv_heads_with_ones(kv[:, KV_WIDTH + j * LANES:KV_WIDTH + (j + 1) * LANES])
        vd_ref[0, :, (2 * j) * LANES:(2 * j + 1) * LANES] = ve.astype(BF16)
        vd_ref[0, :, (2 * j + 1) * LANES:(2 * j + 2) * LANES] = vo.astype(BF16)


def _ctxkv(ctx, csh1, csc1, g1n, w_kv):
    bn, n, d = ctx.shape
    const2 = lambda b: (0, 0)
    blk = lambda b: (b, 0, 0)
    return pl.pallas_call(
        _ctxkv_kernel,
        grid=(bn,),
        in_specs=[pl.BlockSpec((1, n, d), blk),
                  pl.BlockSpec((1, d), const2),
                  pl.BlockSpec((1, d), const2),
                  pl.BlockSpec((1, d), const2),
                  pl.BlockSpec((d, 2 * KV_WIDTH), const2)],
        out_specs=[pl.BlockSpec((1, n, KV_DUP_WIDTH), blk), pl.BlockSpec((1, n, KV_DUP_WIDTH), blk)],
        out_shape=[jax.ShapeDtypeStruct((bn, n, KV_DUP_WIDTH), BF16)] * 2,
        compiler_params=_params("arbitrary"),
        name="ctxkv",
    )(ctx, csh1, csc1, g1n, w_kv)


def _attn_kernel(sink_ref, q_ref, kp_ref, kc_ref, kn_ref, vp_ref, vc_ref, vn_ref, kx_ref, vx_ref, o_ref):
    n = pl.program_id(1)
    rep = N_HEADS // N_KV_HEADS
    rows = rep * BLOCK
    n_loc = 3 * BLOCK
    r = lax.broadcasted_iota(jnp.int32, (BLOCK, n_loc), 0)
    s = lax.broadcasted_iota(jnp.int32, (BLOCK, n_loc), 1)
    key_pos = (n - 1) * BLOCK + s
    valid = (jnp.abs(s - BLOCK - r) <= WINDOW) & (key_pos >= 0) & (key_pos < SEQ)
    bias = jnp.where(valid, 0.0, MASK_VALUE)
    bias = jnp.concatenate([bias] * rep, axis=0)
    lo = lax.broadcasted_iota(jnp.int32, (BLOCK, LANES), 1) < HEAD_DIM
    rcol = lax.broadcasted_iota(jnp.int32, (rows, 1), 0)
    zero = jnp.zeros((BLOCK, LANES), BF16)

    for kh in range(N_KV_HEADS):
        ls = slice(kh * LANES, (kh + 1) * LANES)
        parts = []
        for j in (2 * kh, 2 * kh + 1):
            qc = q_ref[0, :, j * LANES:(j + 1) * LANES]
            parts += [jnp.where(lo, qc, zero), jnp.where(lo, zero, qc)]
        lhs = jnp.concatenate(parts, axis=0)
        k_all = jnp.concatenate([kp_ref[0, :, ls], kc_ref[0, :, ls], kn_ref[0, :, ls], kx_ref[0, :, ls]], axis=0)
        v_all = jnp.concatenate([vp_ref[0, :, ls], vc_ref[0, :, ls], vn_ref[0, :, ls], vx_ref[0, :, ls]], axis=0)
        logits = lax.dot_general(lhs, k_all, (((1,), (1,)), ((), ())), preferred_element_type=F32)
        logits = jnp.concatenate([logits[:, :n_loc] + bias, logits[:, n_loc:]], axis=1)
        sink = LOG2E * jnp.where(
            rcol < BLOCK, sink_ref[rep * kh],
            jnp.where(rcol < 2 * BLOCK, sink_ref[rep * kh + 1],
                      jnp.where(rcol < 3 * BLOCK, sink_ref[rep * kh + 2], sink_ref[rep * kh + 3])))
        m = jnp.maximum(jnp.max(logits, axis=1, keepdims=True), sink)
        p = jnp.exp2(logits - m).astype(BF16)
        o = jnp.dot(p, v_all, preferred_element_type=F32)
        od = o + jnp.exp2(sink - m)
        for c in range(2):
            g_lo = slice((2 * c) * BLOCK, (2 * c + 1) * BLOCK)
            g_hi = slice((2 * c + 1) * BLOCK, (2 * c + 2) * BLOCK)
            out_lo = o[g_lo] / pltpu.roll(od[g_lo], HEAD_DIM, axis=1)
            out_hi = pltpu.roll(o[g_hi], HEAD_DIM, axis=1) / od[g_hi]
            o_ref[0, :, (2 * kh + c) * LANES:(2 * kh + c + 1) * LANES] = jnp.where(
                lo, out_lo, out_hi).astype(BF16)


def _attn(sink, q, kd, vd, kxd, vxd):
    bn, L, _ = q.shape
    nb = L // BLOCK
    cur = lambda b, n: (b, n, 0)
    prev = lambda b, n: (b, jnp.maximum(n - 1, 0), 0)
    nxt = lambda b, n: (b, jnp.minimum(n + 1, nb - 1), 0)
    per_b = lambda b, n: (b, 0, 0)
    kv_blk = (1, BLOCK, KV_DUP_WIDTH)
    return pl.pallas_call(
        _attn_kernel,
        grid=(bn, nb),
        in_specs=[pl.BlockSpec(memory_space=pltpu.SMEM),
                  pl.BlockSpec((1, BLOCK, B_WIDTH), cur),
                  pl.BlockSpec(kv_blk, prev), pl.BlockSpec(kv_blk, cur), pl.BlockSpec(kv_blk, nxt),
                  pl.BlockSpec(kv_blk, prev), pl.BlockSpec(kv_blk, cur), pl.BlockSpec(kv_blk, nxt),
                  pl.BlockSpec((1, CTX_LEN, KV_DUP_WIDTH), per_b),
                  pl.BlockSpec((1, CTX_LEN, KV_DUP_WIDTH), per_b)],
        out_specs=pl.BlockSpec((1, BLOCK, B_WIDTH), cur),
        out_shape=jax.ShapeDtypeStruct((bn, L, B_WIDTH), BF16),
        compiler_params=_params("arbitrary", "arbitrary"),
        name="attn",
    )(sink, q, kd, kd, kd, vd, vd, vd, kxd, vxd)


def _merge_kernel(x_ref, ya_ref, yb_ref, ga_ref, gb_ref, wa_ref, wb_ref, wo_ref, g1_ref, sh_ref, sc_ref,
                  gn_ref, wr_ref, x1_ref, h2_ref, aff_ref):
    pa = jnp.dot(ya_ref[0], wa_ref[...], preferred_element_type=F32)
    pb = jnp.dot(yb_ref[0], wb_ref[...], preferred_element_type=F32)
    merged = (ga_ref[0].astype(F32) * pa + gb_ref[0].astype(F32) * pb).astype(BF16)
    x1 = x_ref[0] + g1_ref[0] * jnp.dot(merged, wo_ref[...], preferred_element_type=F32)
    x1_ref[0] = x1
    h2 = _rms_modulate(x1, gn_ref[...], sh_ref[0], sc_ref[0]).astype(BF16)
    h2_ref[0] = h2
    logits = jnp.dot(h2, wr_ref[...], preferred_element_type=F32)
    e = jnp.exp(logits - jnp.max(logits, axis=-1, keepdims=True))
    aff_ref[0] = e / jnp.sum(e, axis=-1, keepdims=True)


def _merge(x, ya, yb, ga, gb, wa, wb, wo, g1, sh2, sc2, g2n, wr):
    bn, L, d = x.shape
    tm = TOK_TILE
    tok = lambda b, i: (b, i, 0)
    per_b = lambda b, i: (b, 0, 0)
    const2 = lambda b, i: (0, 0)
    return pl.pallas_call(
        _merge_kernel,
        grid=(bn, L // tm),
        in_specs=[pl.BlockSpec((1, tm, d), tok)] * 5
        + [pl.BlockSpec((d, d), const2)] * 3
        + [pl.BlockSpec((1, 1, d), per_b)] * 3
        + [pl.BlockSpec((1, d), const2), pl.BlockSpec((d, N_EXPERTS), const2)],
        out_specs=[pl.BlockSpec((1, tm, d), tok), pl.BlockSpec((1, tm, d), tok),
                   pl.BlockSpec((1, tm, N_EXPERTS), tok)],
        out_shape=[jax.ShapeDtypeStruct((bn, L, d), F32), jax.ShapeDtypeStruct((bn, L, d), BF16),
                   jax.ShapeDtypeStruct((bn, L, N_EXPERTS), F32)],
        compiler_params=_params("arbitrary", "arbitrary"),
        name="merge",
    )(x, ya, yb, ga, gb, wa, wb, wo, g1, sh2, sc2, g2n, wr)


def _excl_prefix_count(m, upper):
    n = m.shape[1]
    outs = []
    carry = jnp.zeros((m.shape[0], 1), F32)
    for blk in range(n // CUMSUM_BLOCK):
        mb = m[:, blk * CUMSUM_BLOCK:(blk + 1) * CUMSUM_BLOCK]
        outs.append(jnp.dot(mb, upper, preferred_element_type=F32) + carry)
        carry = carry + jnp.sum(mb.astype(F32), axis=1, keepdims=True)
    return jnp.concatenate(outs, axis=1)


def _select_kernel(aff_ref, slot_ref, off_ref):
    x = aff_ref[0]
    one = jnp.ones(x.shape, F32)
    zero = jnp.zeros(x.shape, F32)

    def count(mask):
        return jnp.sum(jnp.where(mask, one, zero), axis=1, keepdims=True)

    def body(i, t):
        cand = t | jnp.left_shift(jnp.int32(1), 30 - i)
        return jnp.where(count(x >= lax.bitcast_convert_type(cand, F32)) >= CAPACITY, cand, t)

    thr = lax.bitcast_convert_type(
        lax.fori_loop(0, 31, body, jnp.zeros((x.shape[0], 1), jnp.int32)), F32)
    gt = x > thr
    eq = x == thr
    need = CAPACITY - count(gt)
    ri = lax.broadcasted_iota(jnp.int32, (CUMSUM_BLOCK, CUMSUM_BLOCK), 0)
    ci = lax.broadcasted_iota(jnp.int32, (CUMSUM_BLOCK, CUMSUM_BLOCK), 1)
    upper = jnp.where(ri < ci, 1.0, 0.0).astype(BF16)
    eq_rank = _excl_prefix_count(jnp.where(eq, one, zero).astype(BF16), upper)
    sel = gt | (eq & (eq_rank < need))
    slot = _excl_prefix_count(jnp.where(sel, one, zero).astype(BF16), upper)
    slot_ref[0] = jnp.where(sel, slot, -1.0).astype(jnp.int32)
    lane = lax.broadcasted_iota(jnp.int32, (x.shape[0], LANES), 1)
    off = jnp.zeros((x.shape[0], LANES), F32)
    for j in range(x.shape[1] // WIN_TOK_TILE):
        off = jnp.where(lane == j, slot[:, j * WIN_TOK_TILE:j * WIN_TOK_TILE + 1], off)
    off_ref[0] = off.astype(jnp.int32)


def _select(aff_t):
    bn, e, n = aff_t.shape
    blk = lambda b: (b, 0, 0)
    return pl.pallas_call(
        _select_kernel,
        grid=(bn,),
        in_specs=[pl.BlockSpec((1, e, n), blk)],
        out_specs=[pl.BlockSpec((1, e, n), blk), pl.BlockSpec((1, e, LANES), blk)],
        out_shape=[jax.ShapeDtypeStruct((bn, e, n), jnp.int32),
                   jax.ShapeDtypeStruct((bn, e, LANES), jnp.int32)],
        compiler_params=_params("arbitrary"),
        name="select",
    )(aff_t)


def _window_plan(off):
    n_tiles = SEQ // WIN_TOK_TILE
    first = off[:, :, :n_tiles]
    end = jnp.concatenate([first[:, :, 1:], jnp.full_like(first[:, :, :1], CAPACITY)], axis=2)
    start = jnp.minimum((first // BF16_SUBLANES) * BF16_SUBLANES, CAPACITY - WIN_SLOTS)
    overflow = jnp.any(end - start > WIN_SLOTS, axis=1).astype(jnp.int32)
    return jnp.swapaxes(start, 1, 2).reshape(-1), overflow.reshape(-1)


GATHER_GROUP = 4


def _gather_kernel(start_ref, ovf_ref, slot_ref, h_ref, xe_ref):
    b = pl.program_id(0)
    j = pl.program_id(1)
    n_tiles = pl.num_programs(1)
    tile = h_ref.shape[1]

    @pl.when(j == 0)
    def _():
        xe_ref[...] = jnp.zeros_like(xe_ref)

    h = h_ref[0]
    slot = slot_ref[0].astype(F32)

    def onehot(e, height, first):
        row = lax.broadcasted_iota(jnp.int32, (height, tile), 0).astype(F32)
        return jnp.where(row == slot[e:e + 1, :] - first, 1.0, 0.0).astype(BF16)

    def accumulate(e, rows, part):
        xe_ref[0, e, rows, :] = (xe_ref[0, e, rows, :].astype(F32) + part).astype(BF16)

    overflow = ovf_ref[b * n_tiles + j] != 0

    @pl.when(jnp.logical_not(overflow))
    def _():
        for e0 in range(0, N_EXPERTS, GATHER_GROUP):
            starts = [pl.multiple_of(start_ref[(b * n_tiles + j) * N_EXPERTS + e], BF16_SUBLANES)
                      for e in range(e0, e0 + GATHER_GROUP)]
            lhs = jnp.concatenate([onehot(e0 + i, WIN_SLOTS, s.astype(F32)) for i, s in enumerate(starts)],
                                  axis=0)
            part = jnp.dot(lhs, h, preferred_element_type=F32)
            for i, s in enumerate(starts):
                accumulate(e0 + i, pl.ds(s, WIN_SLOTS), part[i * WIN_SLOTS:(i + 1) * WIN_SLOTS])

    @pl.when(overflow)
    def _():
        for e in range(N_EXPERTS):
            accumulate(e, slice(None), jnp.dot(onehot(e, CAPACITY, 0.0), h, preferred_element_type=F32))


def _gather(start, overflow, slot, h2):
    bn, n, d = h2.shape
    tile = WIN_TOK_TILE
    return pl.pallas_call(
        _gather_kernel,
        grid_spec=pltpu.PrefetchScalarGridSpec(
            num_scalar_prefetch=2,
            grid=(bn, n // tile),
            in_specs=[pl.BlockSpec((1, N_EXPERTS, tile), lambda b, i, *_: (b, 0, i)),
                      pl.BlockSpec((1, tile, d), lambda b, i, *_: (b, i, 0))],
            out_specs=pl.BlockSpec((1, N_EXPERTS, CAPACITY, d), lambda b, i, *_: (b, 0, 0, 0))),
        out_shape=jax.ShapeDtypeStruct((bn, N_EXPERTS, CAPACITY, d), BF16),
        compiler_params=_params("arbitrary", "arbitrary"),
        name="gather",
    )(start, overflow, slot, h2)


def _ffn_kernel(xe_ref, wg_ref, wu_ref, wd_ref, y_ref):
    xe = xe_ref[0, 0]
    fc = 512
    acc = jnp.zeros((xe.shape[0], wd_ref.shape[2]), F32)
    for j in range(D_EXPERT // fc):
        g = jnp.dot(xe, wg_ref[0, :, j * fc:(j + 1) * fc], preferred_element_type=F32)
        u = jnp.dot(xe, wu_ref[0, :, j * fc:(j + 1) * fc], preferred_element_type=F32)
        a = ((g * jax.nn.sigmoid(g)) * u).astype(BF16)
        acc = acc + jnp.dot(a, wd_ref[0, j * fc:(j + 1) * fc, :], preferred_element_type=F32)
    y_ref[0, 0] = acc.astype(BF16)


def _ffn(xe, wg, wu, wd):
    bn, ne, c, d = xe.shape
    f = wg.shape[2]
    return pl.pallas_call(
        _ffn_kernel,
        grid=(ne, bn),
        in_specs=[pl.BlockSpec((1, 1, c, d), lambda e, b: (b, e, 0, 0)),
                  pl.BlockSpec((1, d, f), lambda e, b: (e, 0, 0)),
                  pl.BlockSpec((1, d, f), lambda e, b: (e, 0, 0)),
                  pl.BlockSpec((1, f, d), lambda e, b: (e, 0, 0))],
        out_specs=pl.BlockSpec((1, 1, c, d), lambda e, b: (b, e, 0, 0)),
        out_shape=jax.ShapeDtypeStruct((bn, ne, c, d), BF16),
        compiler_params=_params("arbitrary", "arbitrary"),
        name="ffn",
    )(xe, wg, wu, wd)


def _combine_kernel(start_ref, ovf_ref, slot_ref, aff_ref, y_ref, x1_ref, g2_ref, gf_ref, o_ref):
    b = pl.program_id(0)
    j = pl.program_id(1)
    n_tiles = pl.num_programs(1)
    tile = slot_ref.shape[1]
    slot = slot_ref[0].astype(F32)
    aff = aff_ref[0]

    def finish(moe):
        xo = x1_ref[0] + g2_ref[0] * moe
        o_ref[0] = (xo * lax.rsqrt(jnp.mean(xo * xo, axis=-1, keepdims=True) + EPS)) * gf_ref[...]

    def weighted_onehot(e, width, first):
        col = lax.broadcasted_iota(jnp.int32, (tile, width), 1).astype(F32)
        return jnp.where(col == slot[:, e:e + 1] - first, aff[:, e:e + 1], 0.0).astype(BF16)

    overflow = ovf_ref[b * n_tiles + j] != 0

    @pl.when(jnp.logical_not(overflow))
    def _():
        lhs, rhs = [], []
        for e in range(N_EXPERTS):
            start = pl.multiple_of(start_ref[(b * n_tiles + j) * N_EXPERTS + e], BF16_SUBLANES)
            lhs.append(weighted_onehot(e, WIN_SLOTS, start.astype(F32)))
            rhs.append(y_ref[0, e, pl.ds(start, WIN_SLOTS), :])
        finish(jnp.dot(jnp.concatenate(lhs, axis=1), jnp.concatenate(rhs, axis=0),
                       preferred_element_type=F32))

    @pl.when(overflow)
    def _():
        moe = jnp.zeros(x1_ref.shape[1:], F32)
        for e in range(N_EXPERTS):
            moe = moe + jnp.dot(weighted_onehot(e, CAPACITY, 0.0), y_ref[0, e], preferred_element_type=F32)
        finish(moe)


def _combine(start, overflow, slot_t, aff, y, x1, g2, gf):
    bn, n, d = x1.shape
    tile = WIN_TOK_TILE
    tok = lambda b, i, *_: (b, i, 0)
    return pl.pallas_call(
        _combine_kernel,
        grid_spec=pltpu.PrefetchScalarGridSpec(
            num_scalar_prefetch=2,
            grid=(bn, n // tile),
            in_specs=[pl.BlockSpec((1, tile, N_EXPERTS), tok),
                      pl.BlockSpec((1, tile, N_EXPERTS), tok),
                      pl.BlockSpec((1, N_EXPERTS, CAPACITY, d), lambda b, i, *_: (b, 0, 0, 0)),
                      pl.BlockSpec((1, tile, d), tok),
                      pl.BlockSpec((1, 1, d), lambda b, i, *_: (b, 0, 0)),
                      pl.BlockSpec((1, d), lambda b, i, *_: (0, 0))],
            out_specs=pl.BlockSpec((1, tile, d), tok)),
        out_shape=jax.ShapeDtypeStruct((bn, n, d), F32),
        compiler_params=_params("arbitrary", "arbitrary"),
        name="combine",
    )(start, overflow, slot_t, aff, y, x1, g2, gf)


def _rope_tables(L):
    pos = jnp.arange(L)
    row = (pos // GRID_W).astype(F32)
    col = (pos % GRID_W).astype(F32)
    half = HEAD_DIM // 2
    freqs = ROPE_BASE ** (-jnp.arange(0, half, 2, dtype=F32) / half)
    ang_r = row[:, None] * freqs[None, :]
    ang_c = col[:, None] * freqs[None, :]
    cos_h = jnp.concatenate([jnp.cos(ang_r), jnp.cos(ang_r), jnp.cos(ang_c), jnp.cos(ang_c)], axis=1)
    sin_h = jnp.concatenate([-jnp.sin(ang_r), jnp.sin(ang_r), -jnp.sin(ang_c), jnp.sin(ang_c)], axis=1)
    return jnp.tile(cos_h, (1, LANES // HEAD_DIM)), jnp.tile(sin_h, (1, LANES // HEAD_DIM))


def kernel(x, c, ctx, c_ctx, w_ada, b_ada, norm1_g, w_in, b_merge_gate, gmlp_ln_g, gmlp_ln_b, w_spatial,
           b_spatial, attn_sink, w_proj_a, w_proj_b, w_out, norm2_g, w_router, w_exp_gate, w_exp_up,
           w_exp_down, final_g):
    bn, L, d = x.shape
    depth = w_ada.shape[0]
    assert depth == 1, "the combine kernel fuses the final rmsnorm, so it must follow the only layer"
    cos_t, sin_t = _rope_tables(L)
    ada_rows = 16
    cc = jnp.zeros((ada_rows, d), F32).at[:bn].set(c).at[bn].set(c_ctx)

    for l in range(depth):
        mod = _ada(cc, w_ada[l], b_ada[l][None, :])
        sh1, sc1, g1, sh2, sc2, g2 = [mod[:bn, None, i * d:(i + 1) * d] for i in range(6)]
        csh1 = mod[bn:bn + 1, 0:d]
        csc1 = mod[bn:bn + 1, d:2 * d]
        w_in_b = w_in[l].astype(BF16)
        g1n = norm1_g[l][None, :]

        ya, q, kd, vd, ga, gb = _inproj(
            x, sh1, sc1, g1n, w_in_b, b_merge_gate[l][None, :], gmlp_ln_g[l][None, :],
            gmlp_ln_b[l][None, :], w_spatial[l].astype(BF16), b_spatial[l].T, cos_t, sin_t)
        kxd, vxd = _ctxkv(ctx, csh1, csc1, g1n, w_in_b[:, OFF_K:OFF_GATE])
        yb = _attn(attn_sink[l], q, kd, vd, kxd, vxd)

        x1, h2, aff = _merge(x, ya, yb, ga, gb, w_proj_a[l].astype(BF16), w_proj_b[l].astype(BF16),
                             w_out[l].astype(BF16), g1, sh2, sc2, norm2_g[l][None, :],
                             w_router[l].astype(BF16))
        slot, off = _select(jnp.swapaxes(aff, 1, 2))
        win_start, win_overflow = _window_plan(off)
        xe = _gather(win_start, win_overflow, slot, h2)
        y = _ffn(xe, w_exp_gate[l].astype(BF16), w_exp_up[l].astype(BF16), w_exp_down[l].astype(BF16))
        x = _combine(win_start, win_overflow, jnp.swapaxes(slot, 1, 2), aff, y, x1, g2, final_g[None, :])
    return x
```

```python
import functools

import jax
import jax.numpy as jnp
from jax import lax
from jax.experimental import pallas as pl
from jax.experimental.pallas import tpu as pltpu

F32 = jnp.float32
BF16 = jnp.bfloat16

D_MODEL = 1024
SEQ = 4096
CTX_LEN = 256
GRID_W = 64
EPS = 1e-6
CHUNK = 128
A_GROUPS = 8
A_WIDTH = 1024
N_HEADS = 16
N_KV_HEADS = 4
HEAD_DIM = 64
B_WIDTH = N_HEADS * HEAD_DIM
KV_WIDTH = N_KV_HEADS * HEAD_DIM
WINDOW = 128
BLOCK = 128
ROPE_BASE = 10000.0
N_EXPERTS = 16
CAPACITY = 2 * SEQ // N_EXPERTS
D_EXPERT = 2048
OFF_U = 0
OFF_VA = OFF_U + A_WIDTH
OFF_Q = OFF_VA + A_WIDTH
OFF_K = OFF_Q + B_WIDTH
OFF_VB = OFF_K + KV_WIDTH
OFF_GATE = OFF_VB + KV_WIDTH
IN_COLS = OFF_GATE + 2 * D_MODEL

LANES = 128
KV_DUP_WIDTH = N_KV_HEADS * LANES
VMEM_LIMIT = 56 * 1024 * 1024
MASK_VALUE = -1e30
LOG2E = 1.4426950408889634

TOK_TILE = 512
WIN_TOK_TILE = 512
WIN_SLOTS = 128
BF16_SUBLANES = 16
CUMSUM_BLOCK = 512


def _params(*sem):
    return pltpu.CompilerParams(dimension_semantics=sem, vmem_limit_bytes=VMEM_LIMIT)


def _gelu_tanh(x):
    return 0.5 * x * (1.0 + jnp.tanh(0.7978845608028654 * (x + 0.044715 * (x * x * x))))


def _rms_modulate(x, gain, shift, scale):
    y = x * lax.rsqrt(jnp.mean(x * x, axis=-1, keepdims=True) + EPS)
    return (y * gain) * (1.0 + scale) + shift


def _dup_kv_heads(x):
    lane = lax.broadcasted_iota(jnp.int32, x.shape, 1)
    lo = lane < HEAD_DIM
    r = pltpu.roll(x, HEAD_DIM, axis=1)
    return jnp.where(lo, x, r), jnp.where(lo, r, x)


def _kv_heads_with_ones(x):
    lane = lax.broadcasted_iota(jnp.int32, x.shape, 1)
    lo = lane < HEAD_DIM
    return jnp.where(lo, x, 1.0), jnp.where(lo, pltpu.roll(x, HEAD_DIM, axis=1), 1.0)


def _ada_kernel(c_ref, w_ref, b_ref, o_ref):
    c = c_ref[...]
    s = c * jax.nn.sigmoid(c)
    o_ref[...] = jnp.dot(s, w_ref[...], preferred_element_type=F32,
                         precision=lax.Precision.HIGHEST) + b_ref[...]


def _ada(cc, w, b):
    rows, d = cc.shape
    cols = w.shape[1]
    tn = 1536
    return pl.pallas_call(
        _ada_kernel,
        grid=(cols // tn,),
        in_specs=[pl.BlockSpec((rows, d), lambda j: (0, 0)),
                  pl.BlockSpec((d, tn), lambda j: (0, j)),
                  pl.BlockSpec((1, tn), lambda j: (0, j))],
        out_specs=pl.BlockSpec((rows, tn), lambda j: (0, j)),
        out_shape=jax.ShapeDtypeStruct((rows, cols), F32),
        compiler_params=_params("arbitrary"),
        name="ada",
    )(cc, w, b)


def _inproj_kernel(x_ref, sh_ref, sc_ref, g_ref, w_ref, bg_ref, lng_ref, lnb_ref, ws_ref, bs_ref,
                   cos_ref, sin_ref, ya_ref, q_ref, kd_ref, vd_ref, ga_ref, gb_ref):
    tm = x_ref.shape[1]
    h = _rms_modulate(x_ref[0], g_ref[...], sh_ref[0], sc_ref[0]).astype(BF16)

    def proj(lo, hi):
        return jnp.dot(h, w_ref[:, lo:hi], preferred_element_type=F32)

    u = _gelu_tanh(proj(OFF_U, OFF_VA))
    v = _gelu_tanh(proj(OFF_VA, OFF_Q))
    mu = jnp.mean(v, axis=-1, keepdims=True)
    vc = v - mu
    var = jnp.mean(vc * vc, axis=-1, keepdims=True)
    vln = ((vc * lax.rsqrt(var + EPS)) * lng_ref[...] + lnb_ref[...]).astype(BF16)
    nc = tm // CHUNK
    gd = A_WIDTH // A_GROUPS
    for g in range(A_GROUPS):
        rhs = jnp.concatenate(
            [vln[c * CHUNK:(c + 1) * CHUNK, g * gd:(g + 1) * gd] for c in range(nc)], axis=1)
        mixed = jnp.dot(ws_ref[g], rhs, preferred_element_type=F32) + bs_ref[:, g:g + 1]
        for c in range(nc):
            ya_ref[0, c * CHUNK:(c + 1) * CHUNK, g * gd:(g + 1) * gd] = (
                u[c * CHUNK:(c + 1) * CHUNK, g * gd:(g + 1) * gd]
                * mixed[:, c * CHUNK:(c + 1) * CHUNK]).astype(BF16)

    cos = cos_ref[...]
    sin = sin_ref[...]
    lane = lax.broadcasted_iota(jnp.int32, (tm, LANES), 1)
    first_half = (lane % 32) < 16

    def rope(t):
        partner = jnp.where(first_half, pltpu.roll(t, LANES - 16, axis=1), pltpu.roll(t, 16, axis=1))
        return t * cos + partner * sin

    q = proj(OFF_Q, OFF_K)
    for j in range(B_WIDTH // LANES):
        q_ref[0, :, j * LANES:(j + 1) * LANES] = (
            rope(q[:, j * LANES:(j + 1) * LANES]) * (HEAD_DIM ** -0.5 * LOG2E)).astype(BF16)
    k = proj(OFF_K, OFF_VB)
    vb = proj(OFF_VB, OFF_GATE)
    for j in range(KV_WIDTH // LANES):
        ke, ko = _dup_kv_heads(rope(k[:, j * LANES:(j + 1) * LANES]))
        kd_ref[0, :, (2 * j) * LANES:(2 * j + 1) * LANES] = ke.astype(BF16)
        kd_ref[0, :, (2 * j + 1) * LANES:(2 * j + 2) * LANES] = ko.astype(BF16)
        ve, vo = _kv_heads_with_ones(vb[:, j * LANES:(j + 1) * LANES])
        vd_ref[0, :, (2 * j) * LANES:(2 * j + 1) * LANES] = ve.astype(BF16)
        vd_ref[0, :, (2 * j + 1) * LANES:(2 * j + 2) * LANES] = vo.astype(BF16)

    ga_ref[0] = jax.nn.sigmoid(proj(OFF_GATE, OFF_GATE + D_MODEL) + bg_ref[:, :D_MODEL]).astype(BF16)
    gb_ref[0] = jax.nn.sigmoid(proj(OFF_GATE + D_MODEL, IN_COLS) + bg_ref[:, D_MODEL:]).astype(BF16)


def _inproj(x, sh1, sc1, g1n, w_in, b_gate, ln_g, ln_b, w_s, b_s_t, cos_t, sin_t):
    bn, L, d = x.shape
    tm = TOK_TILE
    tok = lambda b, i: (b, i, 0)
    per_b = lambda b, i: (b, 0, 0)
    const2 = lambda b, i: (0, 0)
    out = lambda w: jax.ShapeDtypeStruct((bn, L, w), BF16)
    return pl.pallas_call(
        _inproj_kernel,
        grid=(bn, L // tm),
        in_specs=[pl.BlockSpec((1, tm, d), tok),
                  pl.BlockSpec((1, 1, d), per_b),
                  pl.BlockSpec((1, 1, d), per_b),
                  pl.BlockSpec((1, d), const2),
                  pl.BlockSpec((d, IN_COLS), const2),
                  pl.BlockSpec((1, 2 * d), const2),
                  pl.BlockSpec((1, A_WIDTH), const2),
                  pl.BlockSpec((1, A_WIDTH), const2),
                  pl.BlockSpec((A_GROUPS, CHUNK, CHUNK), lambda b, i: (0, 0, 0)),
                  pl.BlockSpec((CHUNK, A_GROUPS), const2),
                  pl.BlockSpec((tm, LANES), lambda b, i: (i, 0)),
                  pl.BlockSpec((tm, LANES), lambda b, i: (i, 0))],
        out_specs=[pl.BlockSpec((1, tm, A_WIDTH), tok),
                   pl.BlockSpec((1, tm, B_WIDTH), tok),
                   pl.BlockSpec((1, tm, KV_DUP_WIDTH), tok),
                   pl.BlockSpec((1, tm, KV_DUP_WIDTH), tok),
                   pl.BlockSpec((1, tm, d), tok),
                   pl.BlockSpec((1, tm, d), tok)],
        out_shape=[out(A_WIDTH), out(B_WIDTH), out(KV_DUP_WIDTH), out(KV_DUP_WIDTH), out(d), out(d)],
        compiler_params=_params("arbitrary", "arbitrary"),
        name="inproj",
    )(x, sh1, sc1, g1n, w_in, b_gate, ln_g, ln_b, w_s, b_s_t, cos_t, sin_t)


def _ctxkv_kernel(x_ref, sh_ref, sc_ref, g_ref, w_ref, kd_ref, vd_ref):
    h = _rms_modulate(x_ref[0], g_ref[...], sh_ref[...], sc_ref[...]).astype(BF16)
    kv = jnp.dot(h, w_ref[...], preferred_element_type=F32)
    for j in range(KV_WIDTH // LANES):
        ke, ko = _dup_kv_heads(kv[:, j * LANES:(j + 1) * LANES])
        kd_ref[0, :, (2 * j) * LANES:(2 * j + 1) * LANES] = ke.astype(BF16)
        kd_ref[0, :, (2 * j + 1) * LANES:(2 * j + 2) * LANES] = ko.astype(BF16)
        ve, vo = _kv_heads_with_ones(kv[:, KV_WIDTH + j * LANES:KV_WIDTH + (j + 1) * LANES])
        vd_ref[0, :, (2 * j) * LANES:(2 * j + 1) * LANES] = ve.astype(BF16)
        vd_ref[0, :, (2 * j + 1) * LANES:(2 * j + 2) * LANES] = vo.astype(BF16)


def _ctxkv(ctx, csh1, csc1, g1n, w_kv):
    bn, n, d = ctx.shape
    const2 = lambda b: (0, 0)
    blk = lambda b: (b, 0, 0)
    return pl.pallas_call(
        _ctxkv_kernel,
        grid=(bn,),
        in_specs=[pl.BlockSpec((1, n, d), blk),
                  pl.BlockSpec((1, d), const2),
                  pl.BlockSpec((1, d), const2),
                  pl.BlockSpec((1, d), const2),
                  pl.BlockSpec((d, 2 * KV_WIDTH), const2)],
        out_specs=[pl.BlockSpec((1, n, KV_DUP_WIDTH), blk), pl.BlockSpec((1, n, KV_DUP_WIDTH), blk)],
        out_shape=[jax.ShapeDtypeStruct((bn, n, KV_DUP_WIDTH), BF16)] * 2,
        compiler_params=_params("arbitrary"),
        name="ctxkv",
    )(ctx, csh1, csc1, g1n, w_kv)


def _attn_kernel(sink_ref, q_ref, kp_ref, kc_ref, kn_ref, vp_ref, vc_ref, vn_ref, kx_ref, vx_ref, o_ref):
    n = pl.program_id(1)
    rep = N_HEADS // N_KV_HEADS
    rows = rep * BLOCK
    n_loc = 3 * BLOCK
    r = lax.broadcasted_iota(jnp.int32, (BLOCK, n_loc), 0)
    s = lax.broadcasted_iota(jnp.int32, (BLOCK, n_loc), 1)
    key_pos = (n - 1) * BLOCK + s
    valid = (jnp.abs(s - BLOCK - r) <= WINDOW) & (key_pos >= 0) & (key_pos < SEQ)
    bias = jnp.where(valid, 0.0, MASK_VALUE)
    bias = jnp.concatenate([bias] * rep, axis=0)
    lo = lax.broadcasted_iota(jnp.int32, (BLOCK, LANES), 1) < HEAD_DIM
    rcol = lax.broadcasted_iota(jnp.int32, (rows, 1), 0)
    zero = jnp.zeros((BLOCK, LANES), BF16)

    for kh in range(N_KV_HEADS):
        ls = slice(kh * LANES, (kh + 1) * LANES)
        parts = []
        for j in (2 * kh, 2 * kh + 1):
            qc = q_ref[0, :, j * LANES:(j + 1) * LANES]
            parts += [jnp.where(lo, qc, zero), jnp.where(lo, zero, qc)]
        lhs = jnp.concatenate(parts, axis=0)
        k_all = jnp.concatenate([kp_ref[0, :, ls], kc_ref[0, :, ls], kn_ref[0, :, ls], kx_ref[0, :, ls]], axis=0)
        v_all = jnp.concatenate([vp_ref[0, :, ls], vc_ref[0, :, ls], vn_ref[0, :, ls], vx_ref[0, :, ls]], axis=0)
        logits = lax.dot_general(lhs, k_all, (((1,), (1,)), ((), ())), preferred_element_type=F32)
        logits = jnp.concatenate([logits[:, :n_loc] + bias, logits[:, n_loc:]], axis=1)
        sink = LOG2E * jnp.where(
            rcol < BLOCK, sink_ref[rep * kh],
            jnp.where(rcol < 2 * BLOCK, sink_ref[rep * kh + 1],
                      jnp.where(rcol < 3 * BLOCK, sink_ref[rep * kh + 2], sink_ref[rep * kh + 3])))
        m = jnp.maximum(jnp.max(logits, axis=1, keepdims=True), sink)
        p = jnp.exp2(logits - m).astype(BF16)
        o = jnp.dot(p, v_all, preferred_element_type=F32)
        od = o + jnp.exp2(sink - m)
        for c in range(2):
            g_lo = slice((2 * c) * BLOCK, (2 * c + 1) * BLOCK)
            g_hi = slice((2 * c + 1) * BLOCK, (2 * c + 2) * BLOCK)
            out_lo = o[g_lo] / pltpu.roll(od[g_lo], HEAD_DIM, axis=1)
            out_hi = pltpu.roll(o[g_hi], HEAD_DIM, axis=1) / od[g_hi]
            o_ref[0, :, (2 * kh + c) * LANES:(2 * kh + c + 1) * LANES] = jnp.where(
                lo, out_lo, out_hi).astype(BF16)


def _attn(sink, q, kd, vd, kxd, vxd):
    bn, L, _ = q.shape
    nb = L // BLOCK
    cur = lambda b, n: (b, n, 0)
    prev = lambda b, n: (b, jnp.maximum(n - 1, 0), 0)
    nxt = lambda b, n: (b, jnp.minimum(n + 1, nb - 1), 0)
    per_b = lambda b, n: (b, 0, 0)
    kv_blk = (1, BLOCK, KV_DUP_WIDTH)
    return pl.pallas_call(
        _attn_kernel,
        grid=(bn, nb),
        in_specs=[pl.BlockSpec(memory_space=pltpu.SMEM),
                  pl.BlockSpec((1, BLOCK, B_WIDTH), cur),
                  pl.BlockSpec(kv_blk, prev), pl.BlockSpec(kv_blk, cur), pl.BlockSpec(kv_blk, nxt),
                  pl.BlockSpec(kv_blk, prev), pl.BlockSpec(kv_blk, cur), pl.BlockSpec(kv_blk, nxt),
                  pl.BlockSpec((1, CTX_LEN, KV_DUP_WIDTH), per_b),
                  pl.BlockSpec((1, CTX_LEN, KV_DUP_WIDTH), per_b)],
        out_specs=pl.BlockSpec((1, BLOCK, B_WIDTH), cur),
        out_shape=jax.ShapeDtypeStruct((bn, L, B_WIDTH), BF16),
        compiler_params=_params("arbitrary", "arbitrary"),
        name="attn",
    )(sink, q, kd, kd, kd, vd, vd, vd, kxd, vxd)


def _merge_kernel(x_ref, ya_ref, yb_ref, ga_ref, gb_ref, wa_ref, wb_ref, wo_ref, g1_ref, sh_ref, sc_ref,
                  gn_ref, wr_ref, x1_ref, h2_ref, aff_ref):
    pa = jnp.dot(ya_ref[0], wa_ref[...], preferred_element_type=F32)
    pb = jnp.dot(yb_ref[0], wb_ref[...], preferred_element_type=F32)
    merged = (ga_ref[0].astype(F32) * pa + gb_ref[0].astype(F32) * pb).astype(BF16)
    x1 = x_ref[0] + g1_ref[0] * jnp.dot(merged, wo_ref[...], preferred_element_type=F32)
    x1_ref[0] = x1
    h2 = _rms_modulate(x1, gn_ref[...], sh_ref[0], sc_ref[0]).astype(BF16)
    h2_ref[0] = h2
    logits = jnp.dot(h2, wr_ref[...], preferred_element_type=F32)
    e = jnp.exp(logits - jnp.max(logits, axis=-1, keepdims=True))
    aff_ref[0] = e / jnp.sum(e, axis=-1, keepdims=True)


def _merge(x, ya, yb, ga, gb, wa, wb, wo, g1, sh2, sc2, g2n, wr):
    bn, L, d = x.shape
    tm = TOK_TILE
    tok = lambda b, i: (b, i, 0)
    per_b = lambda b, i: (b, 0, 0)
    const2 = lambda b, i: (0, 0)
    return pl.pallas_call(
        _merge_kernel,
        grid=(bn, L // tm),
        in_specs=[pl.BlockSpec((1, tm, d), tok)] * 5
        + [pl.BlockSpec((d, d), const2)] * 3
        + [pl.BlockSpec((1, 1, d), per_b)] * 3
        + [pl.BlockSpec((1, d), const2), pl.BlockSpec((d, N_EXPERTS), const2)],
        out_specs=[pl.BlockSpec((1, tm, d), tok), pl.BlockSpec((1, tm, d), tok),
                   pl.BlockSpec((1, tm, N_EXPERTS), tok)],
        out_shape=[jax.ShapeDtypeStruct((bn, L, d), F32), jax.ShapeDtypeStruct((bn, L, d), BF16),
                   jax.ShapeDtypeStruct((bn, L, N_EXPERTS), F32)],
        compiler_params=_params("arbitrary", "arbitrary"),
        name="merge",
    )(x, ya, yb, ga, gb, wa, wb, wo, g1, sh2, sc2, g2n, wr)


def _excl_prefix_count(m, upper):
    n = m.shape[1]
    outs = []
    carry = jnp.zeros((m.shape[0], 1), F32)
    for blk in range(n // CUMSUM_BLOCK):
        mb = m[:, blk * CUMSUM_BLOCK:(blk + 1) * CUMSUM_BLOCK]
        outs.append(jnp.dot(mb, upper, preferred_element_type=F32) + carry)
        carry = carry + jnp.sum(mb.astype(F32), axis=1, keepdims=True)
    return jnp.concatenate(outs, axis=1)


def _select_kernel(aff_ref, slot_ref, off_ref):
    x = aff_ref[0]
    one = jnp.ones(x.shape, F32)
    zero = jnp.zeros(x.shape, F32)

    def count(mask):
        return jnp.sum(jnp.where(mask, one, zero), axis=1, keepdims=True)

    def body(i, t):
        cand = t | jnp.left_shift(jnp.int32(1), 30 - i)
        return jnp.where(count(x >= lax.bitcast_convert_type(cand, F32)) >= CAPACITY, cand, t)

    thr = lax.bitcast_convert_type(
        lax.fori_loop(0, 31, body, jnp.zeros((x.shape[0], 1), jnp.int32)), F32)
    gt = x > thr
    eq = x == thr
    need = CAPACITY - count(gt)
    ri = lax.broadcasted_iota(jnp.int32, (CUMSUM_BLOCK, CUMSUM_BLOCK), 0)
    ci = lax.broadcasted_iota(jnp.int32, (CUMSUM_BLOCK, CUMSUM_BLOCK), 1)
    upper = jnp.where(ri < ci, 1.0, 0.0).astype(BF16)
    eq_rank = _excl_prefix_count(jnp.where(eq, one, zero).astype(BF16), upper)
    sel = gt | (eq & (eq_rank < need))
    slot = _excl_prefix_count(jnp.where(sel, one, zero).astype(BF16), upper)
    slot_ref[0] = jnp.where(sel, slot, -1.0).astype(jnp.int32)
    lane = lax.broadcasted_iota(jnp.int32, (x.shape[0], LANES), 1)
    off = jnp.zeros((x.shape[0], LANES), F32)
    for j in range(x.shape[1] // WIN_TOK_TILE):
        off = jnp.where(lane == j, slot[:, j * WIN_TOK_TILE:j * WIN_TOK_TILE + 1], off)
    off_ref[0] = off.astype(jnp.int32)


def _select(aff_t):
    bn, e, n = aff_t.shape
    blk = lambda b: (b, 0, 0)
    return pl.pallas_call(
        _select_kernel,
        grid=(bn,),
        in_specs=[pl.BlockSpec((1, e, n), blk)],
        out_specs=[pl.BlockSpec((1, e, n), blk), pl.BlockSpec((1, e, LANES), blk)],
        out_shape=[jax.ShapeDtypeStruct((bn, e, n), jnp.int32),
                   jax.ShapeDtypeStruct((bn, e, LANES), jnp.int32)],
        compiler_params=_params("arbitrary"),
        name="select",
    )(aff_t)


def _window_plan(off):
    n_tiles = SEQ // WIN_TOK_TILE
    first = off[:, :, :n_tiles]
    end = jnp.concatenate([first[:, :, 1:], jnp.full_like(first[:, :, :1], CAPACITY)], axis=2)
    start = jnp.minimum((first // BF16_SUBLANES) * BF16_SUBLANES, CAPACITY - WIN_SLOTS)
    overflow = jnp.any(end - start > WIN_SLOTS, axis=1).astype(jnp.int32)
    return jnp.swapaxes(start, 1, 2).reshape(-1), overflow.reshape(-1)


GATHER_GROUP = 4


def _gather_kernel(start_ref, ovf_ref, slot_ref, h_ref, xe_ref):
    b = pl.program_id(0)
    j = pl.program_id(1)
    n_tiles = pl.num_programs(1)
    tile = h_ref.shape[1]

    @pl.when(j == 0)
    def _():
        xe_ref[...] = jnp.zeros_like(xe_ref)

    h = h_ref[0]
    slot = slot_ref[0].astype(F32)

    def onehot(e, height, first):
        row = lax.broadcasted_iota(jnp.int32, (height, tile), 0).astype(F32)
        return jnp.where(row == slot[e:e + 1, :] - first, 1.0, 0.0).astype(BF16)

    def accumulate(e, rows, part):
        xe_ref[0, e, rows, :] = (xe_ref[0, e, rows, :].astype(F32) + part).astype(BF16)

    overflow = ovf_ref[b * n_tiles + j] != 0

    @pl.when(jnp.logical_not(overflow))
    def _():
        for e0 in range(0, N_EXPERTS, GATHER_GROUP):
            starts = [pl.multiple_of(start_ref[(b * n_tiles + j) * N_EXPERTS + e], BF16_SUBLANES)
                      for e in range(e0, e0 + GATHER_GROUP)]
            lhs = jnp.concatenate([onehot(e0 + i, WIN_SLOTS, s.astype(F32)) for i, s in enumerate(starts)],
                                  axis=0)
            part = jnp.dot(lhs, h, preferred_element_type=F32)
            for i, s in enumerate(starts):
                accumulate(e0 + i, pl.ds(s, WIN_SLOTS), part[i * WIN_SLOTS:(i + 1) * WIN_SLOTS])

    @pl.when(overflow)
    def _():
        for e in range(N_EXPERTS):
            accumulate(e, slice(None), jnp.dot(onehot(e, CAPACITY, 0.0), h, preferred_element_type=F32))


def _gather(start, overflow, slot, h2):
    bn, n, d = h2.shape
    tile = WIN_TOK_TILE
    return pl.pallas_call(
        _gather_kernel,
        grid_spec=pltpu.PrefetchScalarGridSpec(
            num_scalar_prefetch=2,
            grid=(bn, n // tile),
            in_specs=[pl.BlockSpec((1, N_EXPERTS, tile), lambda b, i, *_: (b, 0, i)),
                      pl.BlockSpec((1, tile, d), lambda b, i, *_: (b, i, 0))],
            out_specs=pl.BlockSpec((1, N_EXPERTS, CAPACITY, d), lambda b, i, *_: (b, 0, 0, 0))),
        out_shape=jax.ShapeDtypeStruct((bn, N_EXPERTS, CAPACITY, d), BF16),
        compiler_params=_params("arbitrary", "arbitrary"),
        name="gather",
    )(start, overflow, slot, h2)


def _ffn_kernel(xe_ref, wg_ref, wu_ref, wd_ref, y_ref, wg_s, wu_s, wd_s):
    r = pl.program_id(0)
    step = pl.program_id(1)
    n_experts = pl.num_programs(0) - 1
    n_chunks = wg_s.shape[1]

    @pl.when(r < n_experts)
    def _():
        slot = lax.rem(r, 2)
        wg_s[slot, step] = wg_ref[0].astype(BF16)
        wu_s[slot, step] = wu_ref[0].astype(BF16)
        wd_s[slot, step] = wd_ref[0].astype(BF16)

    @pl.when(r == 0)
    def _():
        y_ref[...] = jnp.zeros_like(y_ref)

    @pl.when(r > 0)
    def _():
        slot = lax.rem(r - 1, 2)
        xe = xe_ref[0, 0]
        acc = jnp.zeros((xe.shape[0], wd_s.shape[3]), F32)
        for j in range(n_chunks):
            g = jnp.dot(xe, wg_s[slot, j], preferred_element_type=F32)
            u = jnp.dot(xe, wu_s[slot, j], preferred_element_type=F32)
            a = ((g * jax.nn.sigmoid(g)) * u).astype(BF16)
            acc = acc + jnp.dot(a, wd_s[slot, j], preferred_element_type=F32)
        y_ref[0, 0] = acc.astype(BF16)


def _ffn(xe, wg, wu, wd):
    bn, ne, c, d = xe.shape
    f = wg.shape[2]
    fc = f // bn
    assert f % bn == 0 and fc % LANES == 0
    staged = lambda r: jnp.minimum(r, ne - 1)
    chunk = lambda r, s: jnp.where(r < ne, s, bn - 1)
    computed = lambda r: jnp.maximum(r - 1, 0)
    return pl.pallas_call(
        _ffn_kernel,
        grid=(ne + 1, bn),
        in_specs=[pl.BlockSpec((1, 1, c, d), lambda r, s: (s, computed(r), 0, 0)),
                  pl.BlockSpec((1, d, fc), lambda r, s: (staged(r), 0, chunk(r, s))),
                  pl.BlockSpec((1, d, fc), lambda r, s: (staged(r), 0, chunk(r, s))),
                  pl.BlockSpec((1, fc, d), lambda r, s: (staged(r), chunk(r, s), 0))],
        out_specs=pl.BlockSpec((1, 1, c, d), lambda r, s: (s, jnp.where(r == 0, ne, r - 1), 0, 0)),
        out_shape=jax.ShapeDtypeStruct((bn, ne + 1, c, d), BF16),
        scratch_shapes=[pltpu.VMEM((2, bn, d, fc), BF16), pltpu.VMEM((2, bn, d, fc), BF16),
                        pltpu.VMEM((2, bn, fc, d), BF16)],
        compiler_params=_params("arbitrary", "arbitrary"),
        name="ffn",
    )(xe, wg, wu, wd)


def _combine_kernel(start_ref, ovf_ref, slot_ref, aff_ref, y_ref, x1_ref, g2_ref, gf_ref, o_ref):
    b = pl.program_id(0)
    j = pl.program_id(1)
    n_tiles = pl.num_programs(1)
    tile = slot_ref.shape[1]
    slot = slot_ref[0].astype(F32)
    aff = aff_ref[0]

    def finish(moe):
        xo = x1_ref[0] + g2_ref[0] * moe
        o_ref[0] = (xo * lax.rsqrt(jnp.mean(xo * xo, axis=-1, keepdims=True) + EPS)) * gf_ref[...]

    def weighted_onehot(e, width, first):
        col = lax.broadcasted_iota(jnp.int32, (tile, width), 1).astype(F32)
        return jnp.where(col == slot[:, e:e + 1] - first, aff[:, e:e + 1], 0.0).astype(BF16)

    overflow = ovf_ref[b * n_tiles + j] != 0

    @pl.when(jnp.logical_not(overflow))
    def _():
        lhs, rhs = [], []
        for e in range(N_EXPERTS):
            start = pl.multiple_of(start_ref[(b * n_tiles + j) * N_EXPERTS + e], BF16_SUBLANES)
            lhs.append(weighted_onehot(e, WIN_SLOTS, start.astype(F32)))
            rhs.append(y_ref[0, e, pl.ds(start, WIN_SLOTS), :])
        finish(jnp.dot(jnp.concatenate(lhs, axis=1), jnp.concatenate(rhs, axis=0),
                       preferred_element_type=F32))

    @pl.when(overflow)
    def _():
        moe = jnp.zeros(x1_ref.shape[1:], F32)
        for e in range(N_EXPERTS):
            moe = moe + jnp.dot(weighted_onehot(e, CAPACITY, 0.0), y_ref[0, e], preferred_element_type=F32)
        finish(moe)


def _combine(start, overflow, slot_t, aff, y, x1, g2, gf):
    bn, n, d = x1.shape
    tile = WIN_TOK_TILE
    tok = lambda b, i, *_: (b, i, 0)
    return pl.pallas_call(
        _combine_kernel,
        grid_spec=pltpu.PrefetchScalarGridSpec(
            num_scalar_prefetch=2,
            grid=(bn, n // tile),
            in_specs=[pl.BlockSpec((1, tile, N_EXPERTS), tok),
                      pl.BlockSpec((1, tile, N_EXPERTS), tok),
                      pl.BlockSpec((1, N_EXPERTS, CAPACITY, d), lambda b, i, *_: (b, 0, 0, 0)),
                      pl.BlockSpec((1, tile, d), tok),
                      pl.BlockSpec((1, 1, d), lambda b, i, *_: (b, 0, 0)),
                      pl.BlockSpec((1, d), lambda b, i, *_: (0, 0))],
            out_specs=pl.BlockSpec((1, tile, d), tok)),
        out_shape=jax.ShapeDtypeStruct((bn, n, d), F32),
        compiler_params=_params("arbitrary", "arbitrary"),
        name="combine",
    )(start, overflow, slot_t, aff, y, x1, g2, gf)


def _rope_tables(L):
    pos = jnp.arange(L)
    row = (pos // GRID_W).astype(F32)
    col = (pos % GRID_W).astype(F32)
    half = HEAD_DIM // 2
    freqs = ROPE_BASE ** (-jnp.arange(0, half, 2, dtype=F32) / half)
    ang_r = row[:, None] * freqs[None, :]
    ang_c = col[:, None] * freqs[None, :]
    cos_h = jnp.concatenate([jnp.cos(ang_r), jnp.cos(ang_r), jnp.cos(ang_c), jnp.cos(ang_c)], axis=1)
    sin_h = jnp.concatenate([-jnp.sin(ang_r), jnp.sin(ang_r), -jnp.sin(ang_c), jnp.sin(ang_c)], axis=1)
    return jnp.tile(cos_h, (1, LANES // HEAD_DIM)), jnp.tile(sin_h, (1, LANES // HEAD_DIM))


def kernel(x, c, ctx, c_ctx, w_ada, b_ada, norm1_g, w_in, b_merge_gate, gmlp_ln_g, gmlp_ln_b, w_spatial,
           b_spatial, attn_sink, w_proj_a, w_proj_b, w_out, norm2_g, w_router, w_exp_gate, w_exp_up,
           w_exp_down, final_g):
    bn, L, d = x.shape
    depth = w_ada.shape[0]
    assert depth == 1, "the combine kernel fuses the final rmsnorm, so it must follow the only layer"
    cos_t, sin_t = _rope_tables(L)
    ada_rows = 16
    cc = jnp.zeros((ada_rows, d), F32).at[:bn].set(c).at[bn].set(c_ctx)

    for l in range(depth):
        mod = _ada(cc, w_ada[l], b_ada[l][None, :])
        sh1, sc1, g1, sh2, sc2, g2 = [mod[:bn, None, i * d:(i + 1) * d] for i in range(6)]
        csh1 = mod[bn:bn + 1, 0:d]
        csc1 = mod[bn:bn + 1, d:2 * d]
        w_in_b = w_in[l].astype(BF16)
        g1n = norm1_g[l][None, :]

        ya, q, kd, vd, ga, gb = _inproj(
            x, sh1, sc1, g1n, w_in_b, b_merge_gate[l][None, :], gmlp_ln_g[l][None, :],
            gmlp_ln_b[l][None, :], w_spatial[l].astype(BF16), b_spatial[l].T, cos_t, sin_t)
        kxd, vxd = _ctxkv(ctx, csh1, csc1, g1n, w_in_b[:, OFF_K:OFF_GATE])
        yb = _attn(attn_sink[l], q, kd, vd, kxd, vxd)

        x1, h2, aff = _merge(x, ya, yb, ga, gb, w_proj_a[l].astype(BF16), w_proj_b[l].astype(BF16),
                             w_out[l].astype(BF16), g1, sh2, sc2, norm2_g[l][None, :],
                             w_router[l].astype(BF16))
        slot, off = _select(jnp.swapaxes(aff, 1, 2))
        win_start, win_overflow = _window_plan(off)
        xe = _gather(win_start, win_overflow, slot, h2)
        y = _ffn(xe, w_exp_gate[l], w_exp_up[l], w_exp_down[l])
        x = _combine(win_start, win_overflow, jnp.swapaxes(slot, 1, 2), aff, y, x1, g2, final_g[None, :])
    return x
```

```python
import functools

import jax
import jax.numpy as jnp
from jax import lax
from jax.experimental import pallas as pl
from jax.experimental.pallas import tpu as pltpu

F32 = jnp.float32
BF16 = jnp.bfloat16

D_MODEL = 1024
SEQ = 4096
CTX_LEN = 256
GRID_W = 64
EPS = 1e-6
CHUNK = 128
A_GROUPS = 8
A_WIDTH = 1024
N_HEADS = 16
N_KV_HEADS = 4
HEAD_DIM = 64
B_WIDTH = N_HEADS * HEAD_DIM
KV_WIDTH = N_KV_HEADS * HEAD_DIM
WINDOW = 128
BLOCK = 128
ROPE_BASE = 10000.0
N_EXPERTS = 16
CAPACITY = 2 * SEQ // N_EXPERTS
D_EXPERT = 2048
OFF_U = 0
OFF_VA = OFF_U + A_WIDTH
OFF_Q = OFF_VA + A_WIDTH
OFF_K = OFF_Q + B_WIDTH
OFF_VB = OFF_K + KV_WIDTH
OFF_GATE = OFF_VB + KV_WIDTH
IN_COLS = OFF_GATE + 2 * D_MODEL

LANES = 128
KV_DUP_WIDTH = N_KV_HEADS * LANES
VMEM_LIMIT = 56 * 1024 * 1024
MASK_VALUE = -1e30
LOG2E = 1.4426950408889634

TOK_TILE = 512
ATTN_QBLOCKS = 2
WIN_TOK_TILE = 512
WIN_SLOTS = 128
BF16_SUBLANES = 16
CUMSUM_BLOCK = 512


def _params(*sem):
    return pltpu.CompilerParams(dimension_semantics=sem, vmem_limit_bytes=VMEM_LIMIT)


def _gelu_tanh(x):
    return 0.5 * x * (1.0 + jnp.tanh(0.7978845608028654 * (x + 0.044715 * (x * x * x))))


def _rms_modulate(x, gain, shift, scale):
    y = x * lax.rsqrt(jnp.mean(x * x, axis=-1, keepdims=True) + EPS)
    return (y * gain) * (1.0 + scale) + shift


def _dup_kv_heads(x):
    lane = lax.broadcasted_iota(jnp.int32, x.shape, 1)
    lo = lane < HEAD_DIM
    r = pltpu.roll(x, HEAD_DIM, axis=1)
    return jnp.where(lo, x, r), jnp.where(lo, r, x)


def _kv_heads_with_ones(x):
    lane = lax.broadcasted_iota(jnp.int32, x.shape, 1)
    lo = lane < HEAD_DIM
    return jnp.where(lo, x, 1.0), jnp.where(lo, pltpu.roll(x, HEAD_DIM, axis=1), 1.0)


def _ada_kernel(c_ref, w_ref, b_ref, o_ref):
    c = c_ref[...]
    s = c * jax.nn.sigmoid(c)
    o_ref[...] = jnp.dot(s, w_ref[...], preferred_element_type=F32,
                         precision=lax.Precision.HIGHEST) + b_ref[...]


def _ada(cc, w, b):
    rows, d = cc.shape
    cols = w.shape[1]
    tn = 1536
    return pl.pallas_call(
        _ada_kernel,
        grid=(cols // tn,),
        in_specs=[pl.BlockSpec((rows, d), lambda j: (0, 0)),
                  pl.BlockSpec((d, tn), lambda j: (0, j)),
                  pl.BlockSpec((1, tn), lambda j: (0, j))],
        out_specs=pl.BlockSpec((rows, tn), lambda j: (0, j)),
        out_shape=jax.ShapeDtypeStruct((rows, cols), F32),
        compiler_params=_params("arbitrary"),
        name="ada",
    )(cc, w, b)


def _inproj_kernel(x_ref, sh_ref, sc_ref, g_ref, w_ref, bg_ref, lng_ref, lnb_ref, ws_ref, bs_ref,
                   cos_ref, sin_ref, ya_ref, q_ref, kd_ref, vd_ref, ga_ref, gb_ref):
    tm = x_ref.shape[1]
    h = _rms_modulate(x_ref[0], g_ref[...], sh_ref[0], sc_ref[0]).astype(BF16)

    def proj(lo, hi):
        return jnp.dot(h, w_ref[:, lo:hi], preferred_element_type=F32)

    u = _gelu_tanh(proj(OFF_U, OFF_VA))
    v = _gelu_tanh(proj(OFF_VA, OFF_Q))
    mu = jnp.mean(v, axis=-1, keepdims=True)
    vc = v - mu
    var = jnp.mean(vc * vc, axis=-1, keepdims=True)
    vln = ((vc * lax.rsqrt(var + EPS)) * lng_ref[...] + lnb_ref[...]).astype(BF16)
    nc = tm // CHUNK
    gd = A_WIDTH // A_GROUPS
    for g in range(A_GROUPS):
        rhs = jnp.concatenate(
            [vln[c * CHUNK:(c + 1) * CHUNK, g * gd:(g + 1) * gd] for c in range(nc)], axis=1)
        mixed = jnp.dot(ws_ref[g], rhs, preferred_element_type=F32) + bs_ref[:, g:g + 1]
        for c in range(nc):
            ya_ref[0, c * CHUNK:(c + 1) * CHUNK, g * gd:(g + 1) * gd] = (
                u[c * CHUNK:(c + 1) * CHUNK, g * gd:(g + 1) * gd]
                * mixed[:, c * CHUNK:(c + 1) * CHUNK]).astype(BF16)

    cos = cos_ref[...]
    sin = sin_ref[...]
    lane = lax.broadcasted_iota(jnp.int32, (tm, LANES), 1)
    first_half = (lane % 32) < 16

    def rope(t):
        partner = jnp.where(first_half, pltpu.roll(t, LANES - 16, axis=1), pltpu.roll(t, 16, axis=1))
        return t * cos + partner * sin

    q = proj(OFF_Q, OFF_K)
    for j in range(B_WIDTH // LANES):
        q_ref[0, :, j * LANES:(j + 1) * LANES] = (
            rope(q[:, j * LANES:(j + 1) * LANES]) * (HEAD_DIM ** -0.5 * LOG2E)).astype(BF16)
    k = proj(OFF_K, OFF_VB)
    vb = proj(OFF_VB, OFF_GATE)
    for j in range(KV_WIDTH // LANES):
        ke, ko = _dup_kv_heads(rope(k[:, j * LANES:(j + 1) * LANES]))
        kd_ref[0, :, (2 * j) * LANES:(2 * j + 1) * LANES] = ke.astype(BF16)
        kd_ref[0, :, (2 * j + 1) * LANES:(2 * j + 2) * LANES] = ko.astype(BF16)
        ve, vo = _kv_heads_with_ones(vb[:, j * LANES:(j + 1) * LANES])
        vd_ref[0, :, (2 * j) * LANES:(2 * j + 1) * LANES] = ve.astype(BF16)
        vd_ref[0, :, (2 * j + 1) * LANES:(2 * j + 2) * LANES] = vo.astype(BF16)

    ga_ref[0] = jax.nn.sigmoid(proj(OFF_GATE, OFF_GATE + D_MODEL) + bg_ref[:, :D_MODEL]).astype(BF16)
    gb_ref[0] = jax.nn.sigmoid(proj(OFF_GATE + D_MODEL, IN_COLS) + bg_ref[:, D_MODEL:]).astype(BF16)


def _inproj(x, sh1, sc1, g1n, w_in, b_gate, ln_g, ln_b, w_s, b_s_t, cos_t, sin_t):
    bn, L, d = x.shape
    tm = TOK_TILE
    tok = lambda b, i: (b, i, 0)
    per_b = lambda b, i: (b, 0, 0)
    const2 = lambda b, i: (0, 0)
    out = lambda w: jax.ShapeDtypeStruct((bn, L, w), BF16)
    return pl.pallas_call(
        _inproj_kernel,
        grid=(bn, L // tm),
        in_specs=[pl.BlockSpec((1, tm, d), tok),
                  pl.BlockSpec((1, 1, d), per_b),
                  pl.BlockSpec((1, 1, d), per_b),
                  pl.BlockSpec((1, d), const2),
                  pl.BlockSpec((d, IN_COLS), const2),
                  pl.BlockSpec((1, 2 * d), const2),
                  pl.BlockSpec((1, A_WIDTH), const2),
                  pl.BlockSpec((1, A_WIDTH), const2),
                  pl.BlockSpec((A_GROUPS, CHUNK, CHUNK), lambda b, i: (0, 0, 0)),
                  pl.BlockSpec((CHUNK, A_GROUPS), const2),
                  pl.BlockSpec((tm, LANES), lambda b, i: (i, 0)),
                  pl.BlockSpec((tm, LANES), lambda b, i: (i, 0))],
        out_specs=[pl.BlockSpec((1, tm, A_WIDTH), tok),
                   pl.BlockSpec((1, tm, B_WIDTH), tok),
                   pl.BlockSpec((1, tm, KV_DUP_WIDTH), tok),
                   pl.BlockSpec((1, tm, KV_DUP_WIDTH), tok),
                   pl.BlockSpec((1, tm, d), tok),
                   pl.BlockSpec((1, tm, d), tok)],
        out_shape=[out(A_WIDTH), out(B_WIDTH), out(KV_DUP_WIDTH), out(KV_DUP_WIDTH), out(d), out(d)],
        compiler_params=_params("arbitrary", "arbitrary"),
        name="inproj",
    )(x, sh1, sc1, g1n, w_in, b_gate, ln_g, ln_b, w_s, b_s_t, cos_t, sin_t)


def _ctxkv_kernel(x_ref, sh_ref, sc_ref, g_ref, w_ref, kd_ref, vd_ref):
    h = _rms_modulate(x_ref[0], g_ref[...], sh_ref[...], sc_ref[...]).astype(BF16)
    kv = jnp.dot(h, w_ref[...], preferred_element_type=F32)
    for j in range(KV_WIDTH // LANES):
        ke, ko = _dup_kv_heads(kv[:, j * LANES:(j + 1) * LANES])
        kd_ref[0, :, (2 * j) * LANES:(2 * j + 1) * LANES] = ke.astype(BF16)
        kd_ref[0, :, (2 * j + 1) * LANES:(2 * j + 2) * LANES] = ko.astype(BF16)
        ve, vo = _kv_heads_with_ones(kv[:, KV_WIDTH + j * LANES:KV_WIDTH + (j + 1) * LANES])
        vd_ref[0, :, (2 * j) * LANES:(2 * j + 1) * LANES] = ve.astype(BF16)
        vd_ref[0, :, (2 * j + 1) * LANES:(2 * j + 2) * LANES] = vo.astype(BF16)


def _ctxkv(ctx, csh1, csc1, g1n, w_kv):
    bn, n, d = ctx.shape
    const2 = lambda b: (0, 0)
    blk = lambda b: (b, 0, 0)
    return pl.pallas_call(
        _ctxkv_kernel,
        grid=(bn,),
        in_specs=[pl.BlockSpec((1, n, d), blk),
                  pl.BlockSpec((1, d), const2),
                  pl.BlockSpec((1, d), const2),
                  pl.BlockSpec((1, d), const2),
                  pl.BlockSpec((d, 2 * KV_WIDTH), const2)],
        out_specs=[pl.BlockSpec((1, n, KV_DUP_WIDTH), blk), pl.BlockSpec((1, n, KV_DUP_WIDTH), blk)],
        out_shape=[jax.ShapeDtypeStruct((bn, n, KV_DUP_WIDTH), BF16)] * 2,
        compiler_params=_params("arbitrary"),
        name="ctxkv",
    )(ctx, csh1, csc1, g1n, w_kv)


def _attn_kernel(sink_ref, q_ref, kp_ref, km_ref, kn_ref, vp_ref, vm_ref, vn_ref, kx_ref, vx_ref, o_ref):
    rep = N_HEADS // N_KV_HEADS
    rows = rep * BLOCK
    n_loc = 3 * BLOCK
    r = lax.broadcasted_iota(jnp.int32, (BLOCK, n_loc), 0)
    s = lax.broadcasted_iota(jnp.int32, (BLOCK, n_loc), 1)
    in_band = jnp.abs(s - BLOCK - r) <= WINDOW
    lo = lax.broadcasted_iota(jnp.int32, (BLOCK, LANES), 1) < HEAD_DIM
    rcol = lax.broadcasted_iota(jnp.int32, (rows, 1), 0)
    zero = jnp.zeros((BLOCK, LANES), BF16)

    def key_blocks(p_ref, m_ref, n_ref, ls):
        mids = [m_ref[0, i * BLOCK:(i + 1) * BLOCK, ls] for i in range(ATTN_QBLOCKS)]
        return [p_ref[0, :, ls]] + mids + [n_ref[0, :, ls]]

    for sub in range(ATTN_QBLOCKS):
        n = pl.program_id(1) * ATTN_QBLOCKS + sub
        qrows = slice(sub * BLOCK, (sub + 1) * BLOCK)
        key_pos = (n - 1) * BLOCK + s
        bias = jnp.where(in_band & (key_pos >= 0) & (key_pos < SEQ), 0.0, MASK_VALUE)
        bias = jnp.concatenate([bias] * rep, axis=0)
        for kh in range(N_KV_HEADS):
            ls = slice(kh * LANES, (kh + 1) * LANES)
            parts = []
            for j in (2 * kh, 2 * kh + 1):
                qc = q_ref[0, qrows, j * LANES:(j + 1) * LANES]
                parts += [jnp.where(lo, qc, zero), jnp.where(lo, zero, qc)]
            lhs = jnp.concatenate(parts, axis=0)
            k_all = jnp.concatenate(key_blocks(kp_ref, km_ref, kn_ref, ls)[sub:sub + 3] + [kx_ref[0, :, ls]], axis=0)
            v_all = jnp.concatenate(key_blocks(vp_ref, vm_ref, vn_ref, ls)[sub:sub + 3] + [vx_ref[0, :, ls]], axis=0)
            logits = lax.dot_general(lhs, k_all, (((1,), (1,)), ((), ())), preferred_element_type=F32)
            logits = jnp.concatenate([logits[:, :n_loc] + bias, logits[:, n_loc:]], axis=1)
            sink = LOG2E * jnp.where(
                rcol < BLOCK, sink_ref[rep * kh],
                jnp.where(rcol < 2 * BLOCK, sink_ref[rep * kh + 1],
                          jnp.where(rcol < 3 * BLOCK, sink_ref[rep * kh + 2], sink_ref[rep * kh + 3])))
            m = jnp.maximum(jnp.max(logits, axis=1, keepdims=True), sink)
            p = jnp.exp2(logits - m).astype(BF16)
            o = jnp.dot(p, v_all, preferred_element_type=F32)
            od = o + jnp.exp2(sink - m)
            for c in range(2):
                g_lo = slice((2 * c) * BLOCK, (2 * c + 1) * BLOCK)
                g_hi = slice((2 * c + 1) * BLOCK, (2 * c + 2) * BLOCK)
                out_lo = o[g_lo] / pltpu.roll(od[g_lo], HEAD_DIM, axis=1)
                out_hi = pltpu.roll(o[g_hi], HEAD_DIM, axis=1) / od[g_hi]
                o_ref[0, qrows, (2 * kh + c) * LANES:(2 * kh + c + 1) * LANES] = jnp.where(
                    lo, out_lo, out_hi).astype(BF16)


def _attn(sink, q, kd, vd, kxd, vxd):
    bn, L, _ = q.shape
    nb = L // BLOCK
    qb = ATTN_QBLOCKS
    mid = lambda b, i: (b, i, 0)
    prev = lambda b, i: (b, jnp.maximum(i * qb - 1, 0), 0)
    nxt = lambda b, i: (b, jnp.minimum(i * qb + qb, nb - 1), 0)
    per_b = lambda b, i: (b, 0, 0)
    edge_blk = (1, BLOCK, KV_DUP_WIDTH)
    mid_blk = (1, qb * BLOCK, KV_DUP_WIDTH)
    return pl.pallas_call(
        _attn_kernel,
        grid=(bn, nb // qb),
        in_specs=[pl.BlockSpec(memory_space=pltpu.SMEM),
                  pl.BlockSpec((1, qb * BLOCK, B_WIDTH), mid),
                  pl.BlockSpec(edge_blk, prev), pl.BlockSpec(mid_blk, mid), pl.BlockSpec(edge_blk, nxt),
                  pl.BlockSpec(edge_blk, prev), pl.BlockSpec(mid_blk, mid), pl.BlockSpec(edge_blk, nxt),
                  pl.BlockSpec((1, CTX_LEN, KV_DUP_WIDTH), per_b),
                  pl.BlockSpec((1, CTX_LEN, KV_DUP_WIDTH), per_b)],
        out_specs=pl.BlockSpec((1, qb * BLOCK, B_WIDTH), mid),
        out_shape=jax.ShapeDtypeStruct((bn, L, B_WIDTH), BF16),
        compiler_params=_params("arbitrary", "arbitrary"),
        name="attn",
    )(sink, q, kd, kd, kd, vd, vd, vd, kxd, vxd)


def _merge_kernel(x_ref, ya_ref, yb_ref, ga_ref, gb_ref, wa_ref, wb_ref, wo_ref, g1_ref, sh_ref, sc_ref,
                  gn_ref, wr_ref, x1_ref, h2_ref, aff_ref):
    pa = jnp.dot(ya_ref[0], wa_ref[...], preferred_element_type=F32)
    pb = jnp.dot(yb_ref[0], wb_ref[...], preferred_element_type=F32)
    merged = (ga_ref[0].astype(F32) * pa + gb_ref[0].astype(F32) * pb).astype(BF16)
    x1 = x_ref[0] + g1_ref[0] * jnp.dot(merged, wo_ref[...], preferred_element_type=F32)
    x1_ref[0] = x1
    h2 = _rms_modulate(x1, gn_ref[...], sh_ref[0], sc_ref[0]).astype(BF16)
    h2_ref[0] = h2
    logits = jnp.dot(h2, wr_ref[...], preferred_element_type=F32)
    e = jnp.exp(logits - jnp.max(logits, axis=-1, keepdims=True))
    aff_ref[0] = e / jnp.sum(e, axis=-1, keepdims=True)


def _merge(x, ya, yb, ga, gb, wa, wb, wo, g1, sh2, sc2, g2n, wr):
    bn, L, d = x.shape
    tm = TOK_TILE
    tok = lambda b, i: (b, i, 0)
    per_b = lambda b, i: (b, 0, 0)
    const2 = lambda b, i: (0, 0)
    return pl.pallas_call(
        _merge_kernel,
        grid=(bn, L // tm),
        in_specs=[pl.BlockSpec((1, tm, d), tok)] * 5
        + [pl.BlockSpec((d, d), const2)] * 3
        + [pl.BlockSpec((1, 1, d), per_b)] * 3
        + [pl.BlockSpec((1, d), const2), pl.BlockSpec((d, N_EXPERTS), const2)],
        out_specs=[pl.BlockSpec((1, tm, d), tok), pl.BlockSpec((1, tm, d), tok),
                   pl.BlockSpec((1, tm, N_EXPERTS), tok)],
        out_shape=[jax.ShapeDtypeStruct((bn, L, d), F32), jax.ShapeDtypeStruct((bn, L, d), BF16),
                   jax.ShapeDtypeStruct((bn, L, N_EXPERTS), F32)],
        compiler_params=_params("arbitrary", "arbitrary"),
        name="merge",
    )(x, ya, yb, ga, gb, wa, wb, wo, g1, sh2, sc2, g2n, wr)


def _excl_prefix_count(m, upper):
    n = m.shape[1]
    outs = []
    carry = jnp.zeros((m.shape[0], 1), F32)
    for blk in range(n // CUMSUM_BLOCK):
        mb = m[:, blk * CUMSUM_BLOCK:(blk + 1) * CUMSUM_BLOCK]
        outs.append(jnp.dot(mb, upper, preferred_element_type=F32) + carry)
        carry = carry + jnp.sum(mb.astype(F32), axis=1, keepdims=True)
    return jnp.concatenate(outs, axis=1)


def _select_kernel(aff_ref, slot_ref, off_ref):
    x = aff_ref[0]
    one = jnp.ones(x.shape, F32)
    zero = jnp.zeros(x.shape, F32)

    def count(mask):
        return jnp.sum(jnp.where(mask, one, zero), axis=1, keepdims=True)

    def body(i, t):
        cand = t | jnp.left_shift(jnp.int32(1), 30 - i)
        return jnp.where(count(x >= lax.bitcast_convert_type(cand, F32)) >= CAPACITY, cand, t)

    thr = lax.bitcast_convert_type(
        lax.fori_loop(0, 31, body, jnp.zeros((x.shape[0], 1), jnp.int32)), F32)
    gt = x > thr
    eq = x == thr
    need = CAPACITY - count(gt)
    ri = lax.broadcasted_iota(jnp.int32, (CUMSUM_BLOCK, CUMSUM_BLOCK), 0)
    ci = lax.broadcasted_iota(jnp.int32, (CUMSUM_BLOCK, CUMSUM_BLOCK), 1)
    upper = jnp.where(ri < ci, 1.0, 0.0).astype(BF16)
    eq_rank = _excl_prefix_count(jnp.where(eq, one, zero).astype(BF16), upper)
    sel = gt | (eq & (eq_rank < need))
    slot = _excl_prefix_count(jnp.where(sel, one, zero).astype(BF16), upper)
    slot_ref[0] = jnp.where(sel, slot, -1.0).astype(jnp.int32)
    lane = lax.broadcasted_iota(jnp.int32, (x.shape[0], LANES), 1)
    off = jnp.zeros((x.shape[0], LANES), F32)
    for j in range(x.shape[1] // WIN_TOK_TILE):
        off = jnp.where(lane == j, slot[:, j * WIN_TOK_TILE:j * WIN_TOK_TILE + 1], off)
    off_ref[0] = off.astype(jnp.int32)


def _select(aff_t):
    bn, e, n = aff_t.shape
    blk = lambda b: (b, 0, 0)
    return pl.pallas_call(
        _select_kernel,
        grid=(bn,),
        in_specs=[pl.BlockSpec((1, e, n), blk)],
        out_specs=[pl.BlockSpec((1, e, n), blk), pl.BlockSpec((1, e, LANES), blk)],
        out_shape=[jax.ShapeDtypeStruct((bn, e, n), jnp.int32),
                   jax.ShapeDtypeStruct((bn, e, LANES), jnp.int32)],
        compiler_params=_params("arbitrary"),
        name="select",
    )(aff_t)


def _window_plan(off):
    n_tiles = SEQ // WIN_TOK_TILE
    first = off[:, :, :n_tiles]
    end = jnp.concatenate([first[:, :, 1:], jnp.full_like(first[:, :, :1], CAPACITY)], axis=2)
    start = jnp.minimum((first // BF16_SUBLANES) * BF16_SUBLANES, CAPACITY - WIN_SLOTS)
    overflow = jnp.any(end - start > WIN_SLOTS, axis=1).astype(jnp.int32)
    return jnp.swapaxes(start, 1, 2).reshape(-1), overflow.reshape(-1)


GATHER_GROUP = 4


def _gather_kernel(start_ref, ovf_ref, slot_ref, h_ref, xe_ref):
    b = pl.program_id(0)
    j = pl.program_id(1)
    n_tiles = pl.num_programs(1)
    tile = h_ref.shape[1]

    @pl.when(j == 0)
    def _():
        xe_ref[...] = jnp.zeros_like(xe_ref)

    h = h_ref[0]
    slot = slot_ref[0].astype(F32)

    def onehot(e, height, first):
        row = lax.broadcasted_iota(jnp.int32, (height, tile), 0).astype(F32)
        return jnp.where(row == slot[e:e + 1, :] - first, 1.0, 0.0).astype(BF16)

    def accumulate(e, rows, part):
        xe_ref[0, e, rows, :] = (xe_ref[0, e, rows, :].astype(F32) + part).astype(BF16)

    overflow = ovf_ref[b * n_tiles + j] != 0

    @pl.when(jnp.logical_not(overflow))
    def _():
        for e0 in range(0, N_EXPERTS, GATHER_GROUP):
            starts = [pl.multiple_of(start_ref[(b * n_tiles + j) * N_EXPERTS + e], BF16_SUBLANES)
                      for e in range(e0, e0 + GATHER_GROUP)]
            lhs = jnp.concatenate([onehot(e0 + i, WIN_SLOTS, s.astype(F32)) for i, s in enumerate(starts)],
                                  axis=0)
            part = jnp.dot(lhs, h, preferred_element_type=F32)
            for i, s in enumerate(starts):
                accumulate(e0 + i, pl.ds(s, WIN_SLOTS), part[i * WIN_SLOTS:(i + 1) * WIN_SLOTS])

    @pl.when(overflow)
    def _():
        for e in range(N_EXPERTS):
            accumulate(e, slice(None), jnp.dot(onehot(e, CAPACITY, 0.0), h, preferred_element_type=F32))


def _gather(start, overflow, slot, h2):
    bn, n, d = h2.shape
    tile = WIN_TOK_TILE
    return pl.pallas_call(
        _gather_kernel,
        grid_spec=pltpu.PrefetchScalarGridSpec(
            num_scalar_prefetch=2,
            grid=(bn, n // tile),
            in_specs=[pl.BlockSpec((1, N_EXPERTS, tile), lambda b, i, *_: (b, 0, i)),
                      pl.BlockSpec((1, tile, d), lambda b, i, *_: (b, i, 0))],
            out_specs=pl.BlockSpec((1, N_EXPERTS, CAPACITY, d), lambda b, i, *_: (b, 0, 0, 0))),
        out_shape=jax.ShapeDtypeStruct((bn, N_EXPERTS, CAPACITY, d), BF16),
        compiler_params=_params("arbitrary", "arbitrary"),
        name="gather",
    )(start, overflow, slot, h2)


def _ffn_kernel(xe_ref, wg_ref, wu_ref, wd_ref, y_ref, wg_s, wu_s, wd_s):
    r = pl.program_id(0)
    step = pl.program_id(1)
    n_experts = pl.num_programs(0) - 1
    n_chunks = wg_s.shape[1]

    @pl.when(r < n_experts)
    def _():
        slot = lax.rem(r, 2)
        wg_s[slot, step] = wg_ref[0].astype(BF16)
        wu_s[slot, step] = wu_ref[0].astype(BF16)
        wd_s[slot, step] = wd_ref[0].astype(BF16)

    @pl.when(r == 0)
    def _():
        y_ref[...] = jnp.zeros_like(y_ref)

    @pl.when(r > 0)
    def _():
        slot = lax.rem(r - 1, 2)
        xe = xe_ref[0, 0]
        acc = jnp.zeros((xe.shape[0], wd_s.shape[3]), F32)
        for j in range(n_chunks):
            g = jnp.dot(xe, wg_s[slot, j], preferred_element_type=F32)
            u = jnp.dot(xe, wu_s[slot, j], preferred_element_type=F32)
            a = ((g * jax.nn.sigmoid(g)) * u).astype(BF16)
            acc = acc + jnp.dot(a, wd_s[slot, j], preferred_element_type=F32)
        y_ref[0, 0] = acc.astype(BF16)


def _ffn(xe, wg, wu, wd):
    bn, ne, c, d = xe.shape
    f = wg.shape[2]
    fc = f // bn
    assert f % bn == 0 and fc % LANES == 0
    staged = lambda r: jnp.minimum(r, ne - 1)
    chunk = lambda r, s: jnp.where(r < ne, s, bn - 1)
    computed = lambda r: jnp.maximum(r - 1, 0)
    return pl.pallas_call(
        _ffn_kernel,
        grid=(ne + 1, bn),
        in_specs=[pl.BlockSpec((1, 1, c, d), lambda r, s: (s, computed(r), 0, 0)),
                  pl.BlockSpec((1, d, fc), lambda r, s: (staged(r), 0, chunk(r, s))),
                  pl.BlockSpec((1, d, fc), lambda r, s: (staged(r), 0, chunk(r, s))),
                  pl.BlockSpec((1, fc, d), lambda r, s: (staged(r), chunk(r, s), 0))],
        out_specs=pl.BlockSpec((1, 1, c, d), lambda r, s: (s, jnp.where(r == 0, ne, r - 1), 0, 0)),
        out_shape=jax.ShapeDtypeStruct((bn, ne + 1, c, d), BF16),
        scratch_shapes=[pltpu.VMEM((2, bn, d, fc), BF16), pltpu.VMEM((2, bn, d, fc), BF16),
                        pltpu.VMEM((2, bn, fc, d), BF16)],
        compiler_params=_params("arbitrary", "arbitrary"),
        name="ffn",
    )(xe, wg, wu, wd)


def _combine_kernel(start_ref, ovf_ref, slot_ref, aff_ref, y_ref, x1_ref, g2_ref, gf_ref, o_ref):
    b = pl.program_id(0)
    j = pl.program_id(1)
    n_tiles = pl.num_programs(1)
    tile = slot_ref.shape[1]
    slot = slot_ref[0].astype(F32)
    aff = aff_ref[0]

    def finish(moe):
        xo = x1_ref[0] + g2_ref[0] * moe
        o_ref[0] = (xo * lax.rsqrt(jnp.mean(xo * xo, axis=-1, keepdims=True) + EPS)) * gf_ref[...]

    def weighted_onehot(e, width, first):
        col = lax.broadcasted_iota(jnp.int32, (tile, width), 1).astype(F32)
        return jnp.where(col == slot[:, e:e + 1] - first, aff[:, e:e + 1], 0.0).astype(BF16)

    overflow = ovf_ref[b * n_tiles + j] != 0

    @pl.when(jnp.logical_not(overflow))
    def _():
        lhs, rhs = [], []
        for e in range(N_EXPERTS):
            start = pl.multiple_of(start_ref[(b * n_tiles + j) * N_EXPERTS + e], BF16_SUBLANES)
            lhs.append(weighted_onehot(e, WIN_SLOTS, start.astype(F32)))
            rhs.append(y_ref[0, e, pl.ds(start, WIN_SLOTS), :])
        finish(jnp.dot(jnp.concatenate(lhs, axis=1), jnp.concatenate(rhs, axis=0),
                       preferred_element_type=F32))

    @pl.when(overflow)
    def _():
        moe = jnp.zeros(x1_ref.shape[1:], F32)
        for e in range(N_EXPERTS):
            moe = moe + jnp.dot(weighted_onehot(e, CAPACITY, 0.0), y_ref[0, e], preferred_element_type=F32)
        finish(moe)


def _combine(start, overflow, slot_t, aff, y, x1, g2, gf):
    bn, n, d = x1.shape
    tile = WIN_TOK_TILE
    tok = lambda b, i, *_: (b, i, 0)
    return pl.pallas_call(
        _combine_kernel,
        grid_spec=pltpu.PrefetchScalarGridSpec(
            num_scalar_prefetch=2,
            grid=(bn, n // tile),
            in_specs=[pl.BlockSpec((1, tile, N_EXPERTS), tok),
                      pl.BlockSpec((1, tile, N_EXPERTS), tok),
                      pl.BlockSpec((1, N_EXPERTS, CAPACITY, d), lambda b, i, *_: (b, 0, 0, 0)),
                      pl.BlockSpec((1, tile, d), tok),
                      pl.BlockSpec((1, 1, d), lambda b, i, *_: (b, 0, 0)),
                      pl.BlockSpec((1, d), lambda b, i, *_: (0, 0))],
            out_specs=pl.BlockSpec((1, tile, d), tok)),
        out_shape=jax.ShapeDtypeStruct((bn, n, d), F32),
        compiler_params=_params("arbitrary", "arbitrary"),
        name="combine",
    )(start, overflow, slot_t, aff, y, x1, g2, gf)


def _rope_tables(L):
    pos = jnp.arange(L)
    row = (pos // GRID_W).astype(F32)
    col = (pos % GRID_W).astype(F32)
    half = HEAD_DIM // 2
    freqs = ROPE_BASE ** (-jnp.arange(0, half, 2, dtype=F32) / half)
    ang_r = row[:, None] * freqs[None, :]
    ang_c = col[:, None] * freqs[None, :]
    cos_h = jnp.concatenate([jnp.cos(ang_r), jnp.cos(ang_r), jnp.cos(ang_c), jnp.cos(ang_c)], axis=1)
    sin_h = jnp.concatenate([-jnp.sin(ang_r), jnp.sin(ang_r), -jnp.sin(ang_c), jnp.sin(ang_c)], axis=1)
    return jnp.tile(cos_h, (1, LANES // HEAD_DIM)), jnp.tile(sin_h, (1, LANES // HEAD_DIM))


def kernel(x, c, ctx, c_ctx, w_ada, b_ada, norm1_g, w_in, b_merge_gate, gmlp_ln_g, gmlp_ln_b, w_spatial,
           b_spatial, attn_sink, w_proj_a, w_proj_b, w_out, norm2_g, w_router, w_exp_gate, w_exp_up,
           w_exp_down, final_g):
    bn, L, d = x.shape
    depth = w_ada.shape[0]
    assert depth == 1, "the combine kernel fuses the final rmsnorm, so it must follow the only layer"
    cos_t, sin_t = _rope_tables(L)
    ada_rows = 16
    cc = jnp.zeros((ada_rows, d), F32).at[:bn].set(c).at[bn].set(c_ctx)

    for l in range(depth):
        mod = _ada(cc, w_ada[l], b_ada[l][None, :])
        sh1, sc1, g1, sh2, sc2, g2 = [mod[:bn, None, i * d:(i + 1) * d] for i in range(6)]
        csh1 = mod[bn:bn + 1, 0:d]
        csc1 = mod[bn:bn + 1, d:2 * d]
        w_in_b = w_in[l].astype(BF16)
        g1n = norm1_g[l][None, :]

        ya, q, kd, vd, ga, gb = _inproj(
            x, sh1, sc1, g1n, w_in_b, b_merge_gate[l][None, :], gmlp_ln_g[l][None, :],
            gmlp_ln_b[l][None, :], w_spatial[l].astype(BF16), b_spatial[l].T, cos_t, sin_t)
        kxd, vxd = _ctxkv(ctx, csh1, csc1, g1n, w_in_b[:, OFF_K:OFF_GATE])
        yb = _attn(attn_sink[l], q, kd, vd, kxd, vxd)

        x1, h2, aff = _merge(x, ya, yb, ga, gb, w_proj_a[l].astype(BF16), w_proj_b[l].astype(BF16),
                             w_out[l].astype(BF16), g1, sh2, sc2, norm2_g[l][None, :],
                             w_router[l].astype(BF16))
        slot, off = _select(jnp.swapaxes(aff, 1, 2))
        win_start, win_overflow = _window_plan(off)
        xe = _gather(win_start, win_overflow, slot, h2)
        y = _ffn(xe, w_exp_gate[l], w_exp_up[l], w_exp_down[l])
        x = _combine(win_start, win_overflow, jnp.swapaxes(slot, 1, 2), aff, y, x1, g2, final_g[None, :])
    return x
```

```python
import functools

import jax
import jax.numpy as jnp
from jax import lax
from jax.experimental import pallas as pl
from jax.experimental.pallas import tpu as pltpu

F32 = jnp.float32
BF16 = jnp.bfloat16

D_MODEL = 1024
SEQ = 4096
CTX_LEN = 256
GRID_W = 64
EPS = 1e-6
CHUNK = 128
A_GROUPS = 8
A_WIDTH = 1024
N_HEADS = 16
N_KV_HEADS = 4
HEAD_DIM = 64
B_WIDTH = N_HEADS * HEAD_DIM
KV_WIDTH = N_KV_HEADS * HEAD_DIM
WINDOW = 128
BLOCK = 128
ROPE_BASE = 10000.0
N_EXPERTS = 16
CAPACITY = 2 * SEQ // N_EXPERTS
D_EXPERT = 2048
OFF_U = 0
OFF_VA = OFF_U + A_WIDTH
OFF_Q = OFF_VA + A_WIDTH
OFF_K = OFF_Q + B_WIDTH
OFF_VB = OFF_K + KV_WIDTH
OFF_GATE = OFF_VB + KV_WIDTH
IN_COLS = OFF_GATE + 2 * D_MODEL

LANES = 128
KV_DUP_WIDTH = N_KV_HEADS * LANES
VMEM_LIMIT = 56 * 1024 * 1024
MASK_VALUE = -1e30
LOG2E = 1.4426950408889634

TOK_TILE = 512
ATTN_QBLOCKS = 4
WIN_TOK_TILE = 512
WIN_SLOTS = 128
BF16_SUBLANES = 16
CUMSUM_BLOCK = 512


def _params(*sem):
    return pltpu.CompilerParams(dimension_semantics=sem, vmem_limit_bytes=VMEM_LIMIT)


def _gelu_tanh(x):
    return 0.5 * x * (1.0 + jnp.tanh(0.7978845608028654 * (x + 0.044715 * (x * x * x))))


def _rms_modulate(x, gain, shift, scale):
    y = x * lax.rsqrt(jnp.mean(x * x, axis=-1, keepdims=True) + EPS)
    return (y * gain) * (1.0 + scale) + shift


def _dup_kv_heads(x):
    lane = lax.broadcasted_iota(jnp.int32, x.shape, 1)
    lo = lane < HEAD_DIM
    r = pltpu.roll(x, HEAD_DIM, axis=1)
    return jnp.where(lo, x, r), jnp.where(lo, r, x)


def _kv_heads_with_ones(x):
    lane = lax.broadcasted_iota(jnp.int32, x.shape, 1)
    lo = lane < HEAD_DIM
    return jnp.where(lo, x, 1.0), jnp.where(lo, pltpu.roll(x, HEAD_DIM, axis=1), 1.0)


def _ada_kernel(c_ref, w_ref, b_ref, o_ref):
    c = c_ref[...]
    s = c * jax.nn.sigmoid(c)
    o_ref[...] = jnp.dot(s, w_ref[...], preferred_element_type=F32,
                         precision=lax.Precision.HIGHEST) + b_ref[...]


def _ada(cc, w, b):
    rows, d = cc.shape
    cols = w.shape[1]
    tn = 1536
    return pl.pallas_call(
        _ada_kernel,
        grid=(cols // tn,),
        in_specs=[pl.BlockSpec((rows, d), lambda j: (0, 0)),
                  pl.BlockSpec((d, tn), lambda j: (0, j)),
                  pl.BlockSpec((1, tn), lambda j: (0, j))],
        out_specs=pl.BlockSpec((rows, tn), lambda j: (0, j)),
        out_shape=jax.ShapeDtypeStruct((rows, cols), F32),
        compiler_params=_params("arbitrary"),
        name="ada",
    )(cc, w, b)


def _inproj_kernel(x_ref, sh_ref, sc_ref, g_ref, w_ref, bg_ref, lng_ref, lnb_ref, ws_ref, bs_ref,
                   cos_ref, sin_ref, ya_ref, q_ref, kd_ref, vd_ref, ga_ref, gb_ref):
    tm = x_ref.shape[1]
    h = _rms_modulate(x_ref[0], g_ref[...], sh_ref[0], sc_ref[0]).astype(BF16)

    def proj(lo, hi):
        return jnp.dot(h, w_ref[:, lo:hi], preferred_element_type=F32)

    u = _gelu_tanh(proj(OFF_U, OFF_VA))
    v = _gelu_tanh(proj(OFF_VA, OFF_Q))
    mu = jnp.mean(v, axis=-1, keepdims=True)
    vc = v - mu
    var = jnp.mean(vc * vc, axis=-1, keepdims=True)
    vln = ((vc * lax.rsqrt(var + EPS)) * lng_ref[...] + lnb_ref[...]).astype(BF16)
    nc = tm // CHUNK
    gd = A_WIDTH // A_GROUPS
    for g in range(A_GROUPS):
        rhs = jnp.concatenate(
            [vln[c * CHUNK:(c + 1) * CHUNK, g * gd:(g + 1) * gd] for c in range(nc)], axis=1)
        mixed = jnp.dot(ws_ref[g], rhs, preferred_element_type=F32) + bs_ref[:, g:g + 1]
        for c in range(nc):
            ya_ref[0, c * CHUNK:(c + 1) * CHUNK, g * gd:(g + 1) * gd] = (
                u[c * CHUNK:(c + 1) * CHUNK, g * gd:(g + 1) * gd]
                * mixed[:, c * CHUNK:(c + 1) * CHUNK]).astype(BF16)

    cos = cos_ref[...]
    sin = sin_ref[...]
    lane = lax.broadcasted_iota(jnp.int32, (tm, LANES), 1)
    first_half = (lane % 32) < 16

    def rope(t):
        partner = jnp.where(first_half, pltpu.roll(t, LANES - 16, axis=1), pltpu.roll(t, 16, axis=1))
        return t * cos + partner * sin

    q = proj(OFF_Q, OFF_K)
    for j in range(B_WIDTH // LANES):
        q_ref[0, :, j * LANES:(j + 1) * LANES] = (
            rope(q[:, j * LANES:(j + 1) * LANES]) * (HEAD_DIM ** -0.5 * LOG2E)).astype(BF16)
    k = proj(OFF_K, OFF_VB)
    vb = proj(OFF_VB, OFF_GATE)
    for j in range(KV_WIDTH // LANES):
        ke, ko = _dup_kv_heads(rope(k[:, j * LANES:(j + 1) * LANES]))
        kd_ref[0, :, (2 * j) * LANES:(2 * j + 1) * LANES] = ke.astype(BF16)
        kd_ref[0, :, (2 * j + 1) * LANES:(2 * j + 2) * LANES] = ko.astype(BF16)
        ve, vo = _kv_heads_with_ones(vb[:, j * LANES:(j + 1) * LANES])
        vd_ref[0, :, (2 * j) * LANES:(2 * j + 1) * LANES] = ve.astype(BF16)
        vd_ref[0, :, (2 * j + 1) * LANES:(2 * j + 2) * LANES] = vo.astype(BF16)

    ga_ref[0] = jax.nn.sigmoid(proj(OFF_GATE, OFF_GATE + D_MODEL) + bg_ref[:, :D_MODEL]).astype(BF16)
    gb_ref[0] = jax.nn.sigmoid(proj(OFF_GATE + D_MODEL, IN_COLS) + bg_ref[:, D_MODEL:]).astype(BF16)


def _inproj(x, sh1, sc1, g1n, w_in, b_gate, ln_g, ln_b, w_s, b_s_t, cos_t, sin_t):
    bn, L, d = x.shape
    tm = TOK_TILE
    tok = lambda b, i: (b, i, 0)
    per_b = lambda b, i: (b, 0, 0)
    const2 = lambda b, i: (0, 0)
    out = lambda w: jax.ShapeDtypeStruct((bn, L, w), BF16)
    return pl.pallas_call(
        _inproj_kernel,
        grid=(bn, L // tm),
        in_specs=[pl.BlockSpec((1, tm, d), tok),
                  pl.BlockSpec((1, 1, d), per_b),
                  pl.BlockSpec((1, 1, d), per_b),
                  pl.BlockSpec((1, d), const2),
                  pl.BlockSpec((d, IN_COLS), const2),
                  pl.BlockSpec((1, 2 * d), const2),
                  pl.BlockSpec((1, A_WIDTH), const2),
                  pl.BlockSpec((1, A_WIDTH), const2),
                  pl.BlockSpec((A_GROUPS, CHUNK, CHUNK), lambda b, i: (0, 0, 0)),
                  pl.BlockSpec((CHUNK, A_GROUPS), const2),
                  pl.BlockSpec((tm, LANES), lambda b, i: (i, 0)),
                  pl.BlockSpec((tm, LANES), lambda b, i: (i, 0))],
        out_specs=[pl.BlockSpec((1, tm, A_WIDTH), tok),
                   pl.BlockSpec((1, tm, B_WIDTH), tok),
                   pl.BlockSpec((1, tm, KV_DUP_WIDTH), tok),
                   pl.BlockSpec((1, tm, KV_DUP_WIDTH), tok),
                   pl.BlockSpec((1, tm, d), tok),
                   pl.BlockSpec((1, tm, d), tok)],
        out_shape=[out(A_WIDTH), out(B_WIDTH), out(KV_DUP_WIDTH), out(KV_DUP_WIDTH), out(d), out(d)],
        compiler_params=_params("arbitrary", "arbitrary"),
        name="inproj",
    )(x, sh1, sc1, g1n, w_in, b_gate, ln_g, ln_b, w_s, b_s_t, cos_t, sin_t)


def _ctxkv_kernel(x_ref, sh_ref, sc_ref, g_ref, w_ref, kd_ref, vd_ref):
    h = _rms_modulate(x_ref[0], g_ref[...], sh_ref[...], sc_ref[...]).astype(BF16)
    kv = jnp.dot(h, w_ref[...], preferred_element_type=F32)
    for j in range(KV_WIDTH // LANES):
        ke, ko = _dup_kv_heads(kv[:, j * LANES:(j + 1) * LANES])
        kd_ref[0, :, (2 * j) * LANES:(2 * j + 1) * LANES] = ke.astype(BF16)
        kd_ref[0, :, (2 * j + 1) * LANES:(2 * j + 2) * LANES] = ko.astype(BF16)
        ve, vo = _kv_heads_with_ones(kv[:, KV_WIDTH + j * LANES:KV_WIDTH + (j + 1) * LANES])
        vd_ref[0, :, (2 * j) * LANES:(2 * j + 1) * LANES] = ve.astype(BF16)
        vd_ref[0, :, (2 * j + 1) * LANES:(2 * j + 2) * LANES] = vo.astype(BF16)


def _ctxkv(ctx, csh1, csc1, g1n, w_kv):
    bn, n, d = ctx.shape
    const2 = lambda b: (0, 0)
    blk = lambda b: (b, 0, 0)
    return pl.pallas_call(
        _ctxkv_kernel,
        grid=(bn,),
        in_specs=[pl.BlockSpec((1, n, d), blk),
                  pl.BlockSpec((1, d), const2),
                  pl.BlockSpec((1, d), const2),
                  pl.BlockSpec((1, d), const2),
                  pl.BlockSpec((d, 2 * KV_WIDTH), const2)],
        out_specs=[pl.BlockSpec((1, n, KV_DUP_WIDTH), blk), pl.BlockSpec((1, n, KV_DUP_WIDTH), blk)],
        out_shape=[jax.ShapeDtypeStruct((bn, n, KV_DUP_WIDTH), BF16)] * 2,
        compiler_params=_params("arbitrary"),
        name="ctxkv",
    )(ctx, csh1, csc1, g1n, w_kv)


def _attn_kernel(sink_ref, q_ref, kp_ref, km_ref, kn_ref, vp_ref, vm_ref, vn_ref, kx_ref, vx_ref, o_ref):
    rep = N_HEADS // N_KV_HEADS
    rows = rep * BLOCK
    n_loc = 3 * BLOCK
    r = lax.broadcasted_iota(jnp.int32, (BLOCK, n_loc), 0)
    s = lax.broadcasted_iota(jnp.int32, (BLOCK, n_loc), 1)
    in_band = jnp.abs(s - BLOCK - r) <= WINDOW
    lo = lax.broadcasted_iota(jnp.int32, (BLOCK, LANES), 1) < HEAD_DIM
    rcol = lax.broadcasted_iota(jnp.int32, (rows, 1), 0)
    zero = jnp.zeros((BLOCK, LANES), BF16)

    def key_blocks(p_ref, m_ref, n_ref, ls):
        mids = [m_ref[0, i * BLOCK:(i + 1) * BLOCK, ls] for i in range(ATTN_QBLOCKS)]
        return [p_ref[0, :, ls]] + mids + [n_ref[0, :, ls]]

    for sub in range(ATTN_QBLOCKS):
        n = pl.program_id(1) * ATTN_QBLOCKS + sub
        qrows = slice(sub * BLOCK, (sub + 1) * BLOCK)
        key_pos = (n - 1) * BLOCK + s
        bias = jnp.where(in_band & (key_pos >= 0) & (key_pos < SEQ), 0.0, MASK_VALUE)
        bias = jnp.concatenate([bias] * rep, axis=0)
        for kh in range(N_KV_HEADS):
            ls = slice(kh * LANES, (kh + 1) * LANES)
            parts = []
            for j in (2 * kh, 2 * kh + 1):
                qc = q_ref[0, qrows, j * LANES:(j + 1) * LANES]
                parts += [jnp.where(lo, qc, zero), jnp.where(lo, zero, qc)]
            lhs = jnp.concatenate(parts, axis=0)
            k_all = jnp.concatenate(key_blocks(kp_ref, km_ref, kn_ref, ls)[sub:sub + 3] + [kx_ref[0, :, ls]], axis=0)
            v_all = jnp.concatenate(key_blocks(vp_ref, vm_ref, vn_ref, ls)[sub:sub + 3] + [vx_ref[0, :, ls]], axis=0)
            logits = lax.dot_general(lhs, k_all, (((1,), (1,)), ((), ())), preferred_element_type=F32)
            logits = jnp.concatenate([logits[:, :n_loc] + bias, logits[:, n_loc:]], axis=1)
            sink = LOG2E * jnp.where(
                rcol < BLOCK, sink_ref[rep * kh],
                jnp.where(rcol < 2 * BLOCK, sink_ref[rep * kh + 1],
                          jnp.where(rcol < 3 * BLOCK, sink_ref[rep * kh + 2], sink_ref[rep * kh + 3])))
            m = jnp.maximum(jnp.max(logits, axis=1, keepdims=True), sink)
            p = jnp.exp2(logits - m).astype(BF16)
            o = jnp.dot(p, v_all, preferred_element_type=F32)
            od = o + jnp.exp2(sink - m)
            for c in range(2):
                g_lo = slice((2 * c) * BLOCK, (2 * c + 1) * BLOCK)
                g_hi = slice((2 * c + 1) * BLOCK, (2 * c + 2) * BLOCK)
                out_lo = o[g_lo] / pltpu.roll(od[g_lo], HEAD_DIM, axis=1)
                out_hi = pltpu.roll(o[g_hi], HEAD_DIM, axis=1) / od[g_hi]
                o_ref[0, qrows, (2 * kh + c) * LANES:(2 * kh + c + 1) * LANES] = jnp.where(
                    lo, out_lo, out_hi).astype(BF16)


def _attn(sink, q, kd, vd, kxd, vxd):
    bn, L, _ = q.shape
    nb = L // BLOCK
    qb = ATTN_QBLOCKS
    mid = lambda b, i: (b, i, 0)
    prev = lambda b, i: (b, jnp.maximum(i * qb - 1, 0), 0)
    nxt = lambda b, i: (b, jnp.minimum(i * qb + qb, nb - 1), 0)
    per_b = lambda b, i: (b, 0, 0)
    edge_blk = (1, BLOCK, KV_DUP_WIDTH)
    mid_blk = (1, qb * BLOCK, KV_DUP_WIDTH)
    return pl.pallas_call(
        _attn_kernel,
        grid=(bn, nb // qb),
        in_specs=[pl.BlockSpec(memory_space=pltpu.SMEM),
                  pl.BlockSpec((1, qb * BLOCK, B_WIDTH), mid),
                  pl.BlockSpec(edge_blk, prev), pl.BlockSpec(mid_blk, mid), pl.BlockSpec(edge_blk, nxt),
                  pl.BlockSpec(edge_blk, prev), pl.BlockSpec(mid_blk, mid), pl.BlockSpec(edge_blk, nxt),
                  pl.BlockSpec((1, CTX_LEN, KV_DUP_WIDTH), per_b),
                  pl.BlockSpec((1, CTX_LEN, KV_DUP_WIDTH), per_b)],
        out_specs=pl.BlockSpec((1, qb * BLOCK, B_WIDTH), mid),
        out_shape=jax.ShapeDtypeStruct((bn, L, B_WIDTH), BF16),
        compiler_params=_params("arbitrary", "arbitrary"),
        name="attn",
    )(sink, q, kd, kd, kd, vd, vd, vd, kxd, vxd)


def _merge_kernel(x_ref, ya_ref, yb_ref, ga_ref, gb_ref, wa_ref, wb_ref, wo_ref, g1_ref, sh_ref, sc_ref,
                  gn_ref, wr_ref, x1_ref, h2_ref, aff_ref):
    pa = jnp.dot(ya_ref[0], wa_ref[...], preferred_element_type=F32)
    pb = jnp.dot(yb_ref[0], wb_ref[...], preferred_element_type=F32)
    merged = (ga_ref[0].astype(F32) * pa + gb_ref[0].astype(F32) * pb).astype(BF16)
    x1 = x_ref[0] + g1_ref[0] * jnp.dot(merged, wo_ref[...], preferred_element_type=F32)
    x1_ref[0] = x1
    h2 = _rms_modulate(x1, gn_ref[...], sh_ref[0], sc_ref[0]).astype(BF16)
    h2_ref[0] = h2
    logits = jnp.dot(h2, wr_ref[...], preferred_element_type=F32)
    e = jnp.exp(logits - jnp.max(logits, axis=-1, keepdims=True))
    aff_ref[0] = e / jnp.sum(e, axis=-1, keepdims=True)


def _merge(x, ya, yb, ga, gb, wa, wb, wo, g1, sh2, sc2, g2n, wr):
    bn, L, d = x.shape
    tm = TOK_TILE
    tok = lambda b, i: (b, i, 0)
    per_b = lambda b, i: (b, 0, 0)
    const2 = lambda b, i: (0, 0)
    return pl.pallas_call(
        _merge_kernel,
        grid=(bn, L // tm),
        in_specs=[pl.BlockSpec((1, tm, d), tok)] * 5
        + [pl.BlockSpec((d, d), const2)] * 3
        + [pl.BlockSpec((1, 1, d), per_b)] * 3
        + [pl.BlockSpec((1, d), const2), pl.BlockSpec((d, N_EXPERTS), const2)],
        out_specs=[pl.BlockSpec((1, tm, d), tok), pl.BlockSpec((1, tm, d), tok),
                   pl.BlockSpec((1, tm, N_EXPERTS), tok)],
        out_shape=[jax.ShapeDtypeStruct((bn, L, d), F32), jax.ShapeDtypeStruct((bn, L, d), BF16),
                   jax.ShapeDtypeStruct((bn, L, N_EXPERTS), F32)],
        compiler_params=_params("arbitrary", "arbitrary"),
        name="merge",
    )(x, ya, yb, ga, gb, wa, wb, wo, g1, sh2, sc2, g2n, wr)


def _excl_prefix_count(m, upper):
    n = m.shape[1]
    outs = []
    carry = jnp.zeros((m.shape[0], 1), F32)
    for blk in range(n // CUMSUM_BLOCK):
        mb = m[:, blk * CUMSUM_BLOCK:(blk + 1) * CUMSUM_BLOCK]
        outs.append(jnp.dot(mb, upper, preferred_element_type=F32) + carry)
        carry = carry + jnp.sum(mb.astype(F32), axis=1, keepdims=True)
    return jnp.concatenate(outs, axis=1)


def _select_kernel(aff_ref, slot_ref, off_ref):
    x = aff_ref[0]
    one = jnp.ones(x.shape, F32)
    zero = jnp.zeros(x.shape, F32)

    def count(mask):
        return jnp.sum(jnp.where(mask, one, zero), axis=1, keepdims=True)

    def body(i, t):
        cand = t | jnp.left_shift(jnp.int32(1), 30 - i)
        return jnp.where(count(x >= lax.bitcast_convert_type(cand, F32)) >= CAPACITY, cand, t)

    thr = lax.bitcast_convert_type(
        lax.fori_loop(0, 31, body, jnp.zeros((x.shape[0], 1), jnp.int32)), F32)
    gt = x > thr
    eq = x == thr
    need = CAPACITY - count(gt)
    ri = lax.broadcasted_iota(jnp.int32, (CUMSUM_BLOCK, CUMSUM_BLOCK), 0)
    ci = lax.broadcasted_iota(jnp.int32, (CUMSUM_BLOCK, CUMSUM_BLOCK), 1)
    upper = jnp.where(ri < ci, 1.0, 0.0).astype(BF16)
    eq_rank = _excl_prefix_count(jnp.where(eq, one, zero).astype(BF16), upper)
    sel = gt | (eq & (eq_rank < need))
    slot = _excl_prefix_count(jnp.where(sel, one, zero).astype(BF16), upper)
    slot_ref[0] = jnp.where(sel, slot, -1.0).astype(jnp.int32)
    lane = lax.broadcasted_iota(jnp.int32, (x.shape[0], LANES), 1)
    off = jnp.zeros((x.shape[0], LANES), F32)
    for j in range(x.shape[1] // WIN_TOK_TILE):
        off = jnp.where(lane == j, slot[:, j * WIN_TOK_TILE:j * WIN_TOK_TILE + 1], off)
    off_ref[0] = off.astype(jnp.int32)


def _select(aff_t):
    bn, e, n = aff_t.shape
    blk = lambda b: (b, 0, 0)
    return pl.pallas_call(
        _select_kernel,
        grid=(bn,),
        in_specs=[pl.BlockSpec((1, e, n), blk)],
        out_specs=[pl.BlockSpec((1, e, n), blk), pl.BlockSpec((1, e, LANES), blk)],
        out_shape=[jax.ShapeDtypeStruct((bn, e, n), jnp.int32),
                   jax.ShapeDtypeStruct((bn, e, LANES), jnp.int32)],
        compiler_params=_params("arbitrary"),
        name="select",
    )(aff_t)


def _window_plan(off):
    n_tiles = SEQ // WIN_TOK_TILE
    first = off[:, :, :n_tiles]
    end = jnp.concatenate([first[:, :, 1:], jnp.full_like(first[:, :, :1], CAPACITY)], axis=2)
    start = jnp.minimum((first // BF16_SUBLANES) * BF16_SUBLANES, CAPACITY - WIN_SLOTS)
    overflow = jnp.any(end - start > WIN_SLOTS, axis=1).astype(jnp.int32)
    return jnp.swapaxes(start, 1, 2).reshape(-1), overflow.reshape(-1)


GATHER_GROUP = 8


def _gather_kernel(start_ref, ovf_ref, slot_ref, h_ref, xe_ref):
    b = pl.program_id(0)
    j = pl.program_id(1)
    n_tiles = pl.num_programs(1)
    tile = h_ref.shape[1]

    @pl.when(j == 0)
    def _():
        xe_ref[...] = jnp.zeros_like(xe_ref)

    h = h_ref[0]
    slot = slot_ref[0].astype(F32)

    def onehot(e, height, first):
        row = lax.broadcasted_iota(jnp.int32, (height, tile), 0).astype(F32)
        return jnp.where(row == slot[e:e + 1, :] - first, 1.0, 0.0).astype(BF16)

    def accumulate(e, rows, part):
        xe_ref[0, e, rows, :] = (xe_ref[0, e, rows, :].astype(F32) + part).astype(BF16)

    overflow = ovf_ref[b * n_tiles + j] != 0

    @pl.when(jnp.logical_not(overflow))
    def _():
        for e0 in range(0, N_EXPERTS, GATHER_GROUP):
            starts = [pl.multiple_of(start_ref[(b * n_tiles + j) * N_EXPERTS + e], BF16_SUBLANES)
                      for e in range(e0, e0 + GATHER_GROUP)]
            lhs = jnp.concatenate([onehot(e0 + i, WIN_SLOTS, s.astype(F32)) for i, s in enumerate(starts)],
                                  axis=0)
            part = jnp.dot(lhs, h, preferred_element_type=F32)
            for i, s in enumerate(starts):
                accumulate(e0 + i, pl.ds(s, WIN_SLOTS), part[i * WIN_SLOTS:(i + 1) * WIN_SLOTS])

    @pl.when(overflow)
    def _():
        for e in range(N_EXPERTS):
            accumulate(e, slice(None), jnp.dot(onehot(e, CAPACITY, 0.0), h, preferred_element_type=F32))


def _gather(start, overflow, slot, h2):
    bn, n, d = h2.shape
    tile = WIN_TOK_TILE
    return pl.pallas_call(
        _gather_kernel,
        grid_spec=pltpu.PrefetchScalarGridSpec(
            num_scalar_prefetch=2,
            grid=(bn, n // tile),
            in_specs=[pl.BlockSpec((1, N_EXPERTS, tile), lambda b, i, *_: (b, 0, i)),
                      pl.BlockSpec((1, tile, d), lambda b, i, *_: (b, i, 0))],
            out_specs=pl.BlockSpec((1, N_EXPERTS, CAPACITY, d), lambda b, i, *_: (b, 0, 0, 0))),
        out_shape=jax.ShapeDtypeStruct((bn, N_EXPERTS, CAPACITY, d), BF16),
        compiler_params=_params("arbitrary", "arbitrary"),
        name="gather",
    )(start, overflow, slot, h2)


def _ffn_kernel(xe_ref, wg_ref, wu_ref, wd_ref, y_ref, wg_s, wu_s, wd_s):
    r = pl.program_id(0)
    step = pl.program_id(1)
    n_experts = pl.num_programs(0) - 1
    n_chunks = wg_s.shape[1]

    @pl.when(r < n_experts)
    def _():
        slot = lax.rem(r, 2)
        wg_s[slot, step] = wg_ref[0].astype(BF16)
        wu_s[slot, step] = wu_ref[0].astype(BF16)
        wd_s[slot, step] = wd_ref[0].astype(BF16)

    @pl.when(r == 0)
    def _():
        y_ref[...] = jnp.zeros_like(y_ref)

    @pl.when(r > 0)
    def _():
        slot = lax.rem(r - 1, 2)
        xe = xe_ref[0, 0]
        acc = jnp.zeros((xe.shape[0], wd_s.shape[3]), F32)
        for j in range(n_chunks):
            g = jnp.dot(xe, wg_s[slot, j], preferred_element_type=F32)
            u = jnp.dot(xe, wu_s[slot, j], preferred_element_type=F32)
            a = ((g * jax.nn.sigmoid(g)) * u).astype(BF16)
            acc = acc + jnp.dot(a, wd_s[slot, j], preferred_element_type=F32)
        y_ref[0, 0] = acc.astype(BF16)


def _ffn(xe, wg, wu, wd):
    bn, ne, c, d = xe.shape
    f = wg.shape[2]
    fc = f // bn
    assert f % bn == 0 and fc % LANES == 0
    staged = lambda r: jnp.minimum(r, ne - 1)
    chunk = lambda r, s: jnp.where(r < ne, s, bn - 1)
    computed = lambda r: jnp.maximum(r - 1, 0)
    return pl.pallas_call(
        _ffn_kernel,
        grid=(ne + 1, bn),
        in_specs=[pl.BlockSpec((1, 1, c, d), lambda r, s: (s, computed(r), 0, 0)),
                  pl.BlockSpec((1, d, fc), lambda r, s: (staged(r), 0, chunk(r, s))),
                  pl.BlockSpec((1, d, fc), lambda r, s: (staged(r), 0, chunk(r, s))),
                  pl.BlockSpec((1, fc, d), lambda r, s: (staged(r), chunk(r, s), 0))],
        out_specs=pl.BlockSpec((1, 1, c, d), lambda r, s: (s, jnp.where(r == 0, ne, r - 1), 0, 0)),
        out_shape=jax.ShapeDtypeStruct((bn, ne + 1, c, d), BF16),
        scratch_shapes=[pltpu.VMEM((2, bn, d, fc), BF16), pltpu.VMEM((2, bn, d, fc), BF16),
                        pltpu.VMEM((2, bn, fc, d), BF16)],
        compiler_params=_params("arbitrary", "arbitrary"),
        name="ffn",
    )(xe, wg, wu, wd)


def _combine_kernel(start_ref, ovf_ref, slot_ref, aff_ref, y_ref, x1_ref, g2_ref, gf_ref, o_ref):
    b = pl.program_id(0)
    j = pl.program_id(1)
    n_tiles = pl.num_programs(1)
    tile = slot_ref.shape[1]
    slot = slot_ref[0].astype(F32)
    aff = aff_ref[0]

    def finish(moe):
        xo = x1_ref[0] + g2_ref[0] * moe
        o_ref[0] = (xo * lax.rsqrt(jnp.mean(xo * xo, axis=-1, keepdims=True) + EPS)) * gf_ref[...]

    def weighted_onehot(e, width, first):
        col = lax.broadcasted_iota(jnp.int32, (tile, width), 1).astype(F32)
        return jnp.where(col == slot[:, e:e + 1] - first, aff[:, e:e + 1], 0.0).astype(BF16)

    overflow = ovf_ref[b * n_tiles + j] != 0

    @pl.when(jnp.logical_not(overflow))
    def _():
        lhs, rhs = [], []
        for e in range(N_EXPERTS):
            start = pl.multiple_of(start_ref[(b * n_tiles + j) * N_EXPERTS + e], BF16_SUBLANES)
            lhs.append(weighted_onehot(e, WIN_SLOTS, start.astype(F32)))
            rhs.append(y_ref[0, e, pl.ds(start, WIN_SLOTS), :])
        finish(jnp.dot(jnp.concatenate(lhs, axis=1), jnp.concatenate(rhs, axis=0),
                       preferred_element_type=F32))

    @pl.when(overflow)
    def _():
        moe = jnp.zeros(x1_ref.shape[1:], F32)
        for e in range(N_EXPERTS):
            moe = moe + jnp.dot(weighted_onehot(e, CAPACITY, 0.0), y_ref[0, e], preferred_element_type=F32)
        finish(moe)


def _combine(start, overflow, slot_t, aff, y, x1, g2, gf):
    bn, n, d = x1.shape
    tile = WIN_TOK_TILE
    tok = lambda b, i, *_: (b, i, 0)
    return pl.pallas_call(
        _combine_kernel,
        grid_spec=pltpu.PrefetchScalarGridSpec(
            num_scalar_prefetch=2,
            grid=(bn, n // tile),
            in_specs=[pl.BlockSpec((1, tile, N_EXPERTS), tok),
                      pl.BlockSpec((1, tile, N_EXPERTS), tok),
                      pl.BlockSpec((1, N_EXPERTS, CAPACITY, d), lambda b, i, *_: (b, 0, 0, 0)),
                      pl.BlockSpec((1, tile, d), tok),
                      pl.BlockSpec((1, 1, d), lambda b, i, *_: (b, 0, 0)),
                      pl.BlockSpec((1, d), lambda b, i, *_: (0, 0))],
            out_specs=pl.BlockSpec((1, tile, d), tok)),
        out_shape=jax.ShapeDtypeStruct((bn, n, d), F32),
        compiler_params=_params("arbitrary", "arbitrary"),
        name="combine",
    )(start, overflow, slot_t, aff, y, x1, g2, gf)


def _rope_tables(L):
    pos = jnp.arange(L)
    row = (pos // GRID_W).astype(F32)
    col = (pos % GRID_W).astype(F32)
    half = HEAD_DIM // 2
    freqs = ROPE_BASE ** (-jnp.arange(0, half, 2, dtype=F32) / half)
    ang_r = row[:, None] * freqs[None, :]
    ang_c = col[:, None] * freqs[None, :]
    cos_h = jnp.concatenate([jnp.cos(ang_r), jnp.cos(ang_r), jnp.cos(ang_c), jnp.cos(ang_c)], axis=1)
    sin_h = jnp.concatenate([-jnp.sin(ang_r), jnp.sin(ang_r), -jnp.sin(ang_c), jnp.sin(ang_c)], axis=1)
    return jnp.tile(cos_h, (1, LANES // HEAD_DIM)), jnp.tile(sin_h, (1, LANES // HEAD_DIM))


def kernel(x, c, ctx, c_ctx, w_ada, b_ada, norm1_g, w_in, b_merge_gate, gmlp_ln_g, gmlp_ln_b, w_spatial,
           b_spatial, attn_sink, w_proj_a, w_proj_b, w_out, norm2_g, w_router, w_exp_gate, w_exp_up,
           w_exp_down, final_g):
    bn, L, d = x.shape
    depth = w_ada.shape[0]
    assert depth == 1, "the combine kernel fuses the final rmsnorm, so it must follow the only layer"
    cos_t, sin_t = _rope_tables(L)
    ada_rows = 16
    cc = jnp.zeros((ada_rows, d), F32).at[:bn].set(c).at[bn].set(c_ctx)

    for l in range(depth):
        mod = _ada(cc, w_ada[l], b_ada[l][None, :])
        sh1, sc1, g1, sh2, sc2, g2 = [mod[:bn, None, i * d:(i + 1) * d] for i in range(6)]
        csh1 = mod[bn:bn + 1, 0:d]
        csc1 = mod[bn:bn + 1, d:2 * d]
        w_in_b = w_in[l].astype(BF16)
        g1n = norm1_g[l][None, :]

        ya, q, kd, vd, ga, gb = _inproj(
            x, sh1, sc1, g1n, w_in_b, b_merge_gate[l][None, :], gmlp_ln_g[l][None, :],
            gmlp_ln_b[l][None, :], w_spatial[l].astype(BF16), b_spatial[l].T, cos_t, sin_t)
        kxd, vxd = _ctxkv(ctx, csh1, csc1, g1n, w_in_b[:, OFF_K:OFF_GATE])
        yb = _attn(attn_sink[l], q, kd, vd, kxd, vxd)

        x1, h2, aff = _merge(x, ya, yb, ga, gb, w_proj_a[l].astype(BF16), w_proj_b[l].astype(BF16),
                             w_out[l].astype(BF16), g1, sh2, sc2, norm2_g[l][None, :],
                             w_router[l].astype(BF16))
        slot, off = _select(jnp.swapaxes(aff, 1, 2))
        win_start, win_overflow = _window_plan(off)
        xe = _gather(win_start, win_overflow, slot, h2)
        y = _ffn(xe, w_exp_gate[l], w_exp_up[l], w_exp_down[l])
        x = _combine(win_start, win_overflow, jnp.swapaxes(slot, 1, 2), aff, y, x1, g2, final_g[None, :])
    return x
```

```python
import functools

import jax
import jax.numpy as jnp
from jax import lax
from jax.experimental import pallas as pl
from jax.experimental.pallas import tpu as pltpu

F32 = jnp.float32
BF16 = jnp.bfloat16

D_MODEL = 1024
SEQ = 4096
CTX_LEN = 256
GRID_W = 64
EPS = 1e-6
CHUNK = 128
A_GROUPS = 8
A_WIDTH = 1024
N_HEADS = 16
N_KV_HEADS = 4
HEAD_DIM = 64
B_WIDTH = N_HEADS * HEAD_DIM
KV_WIDTH = N_KV_HEADS * HEAD_DIM
WINDOW = 128
BLOCK = 128
ROPE_BASE = 10000.0
N_EXPERTS = 16
CAPACITY = 2 * SEQ // N_EXPERTS
D_EXPERT = 2048
OFF_U = 0
OFF_VA = OFF_U + A_WIDTH
OFF_Q = OFF_VA + A_WIDTH
OFF_K = OFF_Q + B_WIDTH
OFF_VB = OFF_K + KV_WIDTH
OFF_GATE = OFF_VB + KV_WIDTH
IN_COLS = OFF_GATE + 2 * D_MODEL

LANES = 128
KV_DUP_WIDTH = N_KV_HEADS * LANES
VMEM_LIMIT = 56 * 1024 * 1024
MASK_VALUE = -1e30
LOG2E = 1.4426950408889634

TOK_TILE = 512
ATTN_QBLOCKS = 8
WIN_TOK_TILE = 512
WIN_SLOTS = 128
BF16_SUBLANES = 16
CUMSUM_BLOCK = 512


def _params(*sem):
    return pltpu.CompilerParams(dimension_semantics=sem, vmem_limit_bytes=VMEM_LIMIT)


def _gelu_tanh(x):
    return 0.5 * x * (1.0 + jnp.tanh(0.7978845608028654 * (x + 0.044715 * (x * x * x))))


def _rms_modulate(x, gain, shift, scale):
    y = x * lax.rsqrt(jnp.mean(x * x, axis=-1, keepdims=True) + EPS)
    return (y * gain) * (1.0 + scale) + shift


def _dup_kv_heads(x):
    lane = lax.broadcasted_iota(jnp.int32, x.shape, 1)
    lo = lane < HEAD_DIM
    r = pltpu.roll(x, HEAD_DIM, axis=1)
    return jnp.where(lo, x, r), jnp.where(lo, r, x)


def _kv_heads_with_ones(x):
    lane = lax.broadcasted_iota(jnp.int32, x.shape, 1)
    lo = lane < HEAD_DIM
    return jnp.where(lo, x, 1.0), jnp.where(lo, pltpu.roll(x, HEAD_DIM, axis=1), 1.0)


def _ada_kernel(c_ref, w_ref, b_ref, o_ref):
    c = c_ref[...]
    s = c * jax.nn.sigmoid(c)
    o_ref[...] = jnp.dot(s, w_ref[...], preferred_element_type=F32,
                         precision=lax.Precision.HIGHEST) + b_ref[...]


def _ada(cc, w, b):
    rows, d = cc.shape
    cols = w.shape[1]
    tn = 1536
    return pl.pallas_call(
        _ada_kernel,
        grid=(cols // tn,),
        in_specs=[pl.BlockSpec((rows, d), lambda j: (0, 0)),
                  pl.BlockSpec((d, tn), lambda j: (0, j)),
                  pl.BlockSpec((1, tn), lambda j: (0, j))],
        out_specs=pl.BlockSpec((rows, tn), lambda j: (0, j)),
        out_shape=jax.ShapeDtypeStruct((rows, cols), F32),
        compiler_params=_params("arbitrary"),
        name="ada",
    )(cc, w, b)


def _inproj_kernel(x_ref, sh_ref, sc_ref, g_ref, w_ref, bg_ref, lng_ref, lnb_ref, ws_ref, bs_ref,
                   cos_ref, sin_ref, ya_ref, q_ref, kd_ref, vd_ref, ga_ref, gb_ref):
    tm = x_ref.shape[1]
    h = _rms_modulate(x_ref[0], g_ref[...], sh_ref[0], sc_ref[0]).astype(BF16)

    def proj(lo, hi):
        return jnp.dot(h, w_ref[:, lo:hi], preferred_element_type=F32)

    u = _gelu_tanh(proj(OFF_U, OFF_VA))
    v = _gelu_tanh(proj(OFF_VA, OFF_Q))
    mu = jnp.mean(v, axis=-1, keepdims=True)
    vc = v - mu
    var = jnp.mean(vc * vc, axis=-1, keepdims=True)
    vln = ((vc * lax.rsqrt(var + EPS)) * lng_ref[...] + lnb_ref[...]).astype(BF16)
    nc = tm // CHUNK
    gd = A_WIDTH // A_GROUPS
    for g in range(A_GROUPS):
        rhs = jnp.concatenate(
            [vln[c * CHUNK:(c + 1) * CHUNK, g * gd:(g + 1) * gd] for c in range(nc)], axis=1)
        mixed = jnp.dot(ws_ref[g], rhs, preferred_element_type=F32) + bs_ref[:, g:g + 1]
        for c in range(nc):
            ya_ref[0, c * CHUNK:(c + 1) * CHUNK, g * gd:(g + 1) * gd] = (
                u[c * CHUNK:(c + 1) * CHUNK, g * gd:(g + 1) * gd]
                * mixed[:, c * CHUNK:(c + 1) * CHUNK]).astype(BF16)

    cos = cos_ref[...]
    sin = sin_ref[...]
    lane = lax.broadcasted_iota(jnp.int32, (tm, LANES), 1)
    first_half = (lane % 32) < 16

    def rope(t):
        partner = jnp.where(first_half, pltpu.roll(t, LANES - 16, axis=1), pltpu.roll(t, 16, axis=1))
        return t * cos + partner * sin

    q = proj(OFF_Q, OFF_K)
    for j in range(B_WIDTH // LANES):
        q_ref[0, :, j * LANES:(j + 1) * LANES] = (
            rope(q[:, j * LANES:(j + 1) * LANES]) * (HEAD_DIM ** -0.5 * LOG2E)).astype(BF16)
    k = proj(OFF_K, OFF_VB)
    vb = proj(OFF_VB, OFF_GATE)
    for j in range(KV_WIDTH // LANES):
        ke, ko = _dup_kv_heads(rope(k[:, j * LANES:(j + 1) * LANES]))
        kd_ref[0, :, (2 * j) * LANES:(2 * j + 1) * LANES] = ke.astype(BF16)
        kd_ref[0, :, (2 * j + 1) * LANES:(2 * j + 2) * LANES] = ko.astype(BF16)
        ve, vo = _kv_heads_with_ones(vb[:, j * LANES:(j + 1) * LANES])
        vd_ref[0, :, (2 * j) * LANES:(2 * j + 1) * LANES] = ve.astype(BF16)
        vd_ref[0, :, (2 * j + 1) * LANES:(2 * j + 2) * LANES] = vo.astype(BF16)

    ga_ref[0] = jax.nn.sigmoid(proj(OFF_GATE, OFF_GATE + D_MODEL) + bg_ref[:, :D_MODEL]).astype(BF16)
    gb_ref[0] = jax.nn.sigmoid(proj(OFF_GATE + D_MODEL, IN_COLS) + bg_ref[:, D_MODEL:]).astype(BF16)


def _inproj(x, sh1, sc1, g1n, w_in, b_gate, ln_g, ln_b, w_s, b_s_t, cos_t, sin_t):
    bn, L, d = x.shape
    tm = TOK_TILE
    tok = lambda b, i: (b, i, 0)
    per_b = lambda b, i: (b, 0, 0)
    const2 = lambda b, i: (0, 0)
    out = lambda w: jax.ShapeDtypeStruct((bn, L, w), BF16)
    return pl.pallas_call(
        _inproj_kernel,
        grid=(bn, L // tm),
        in_specs=[pl.BlockSpec((1, tm, d), tok),
                  pl.BlockSpec((1, 1, d), per_b),
                  pl.BlockSpec((1, 1, d), per_b),
                  pl.BlockSpec((1, d), const2),
                  pl.BlockSpec((d, IN_COLS), const2),
                  pl.BlockSpec((1, 2 * d), const2),
                  pl.BlockSpec((1, A_WIDTH), const2),
                  pl.BlockSpec((1, A_WIDTH), const2),
                  pl.BlockSpec((A_GROUPS, CHUNK, CHUNK), lambda b, i: (0, 0, 0)),
                  pl.BlockSpec((CHUNK, A_GROUPS), const2),
                  pl.BlockSpec((tm, LANES), lambda b, i: (i, 0)),
                  pl.BlockSpec((tm, LANES), lambda b, i: (i, 0))],
        out_specs=[pl.BlockSpec((1, tm, A_WIDTH), tok),
                   pl.BlockSpec((1, tm, B_WIDTH), tok),
                   pl.BlockSpec((1, tm, KV_DUP_WIDTH), tok),
                   pl.BlockSpec((1, tm, KV_DUP_WIDTH), tok),
                   pl.BlockSpec((1, tm, d), tok),
                   pl.BlockSpec((1, tm, d), tok)],
        out_shape=[out(A_WIDTH), out(B_WIDTH), out(KV_DUP_WIDTH), out(KV_DUP_WIDTH), out(d), out(d)],
        compiler_params=_params("arbitrary", "arbitrary"),
        name="inproj",
    )(x, sh1, sc1, g1n, w_in, b_gate, ln_g, ln_b, w_s, b_s_t, cos_t, sin_t)


def _ctxkv_kernel(x_ref, sh_ref, sc_ref, g_ref, w_ref, kd_ref, vd_ref):
    h = _rms_modulate(x_ref[0], g_ref[...], sh_ref[...], sc_ref[...]).astype(BF16)
    kv = jnp.dot(h, w_ref[...], preferred_element_type=F32)
    for j in range(KV_WIDTH // LANES):
        ke, ko = _dup_kv_heads(kv[:, j * LANES:(j + 1) * LANES])
        kd_ref[0, :, (2 * j) * LANES:(2 * j + 1) * LANES] = ke.astype(BF16)
        kd_ref[0, :, (2 * j + 1) * LANES:(2 * j + 2) * LANES] = ko.astype(BF16)
        ve, vo = _kv_heads_with_ones(kv[:, KV_WIDTH + j * LANES:KV_WIDTH + (j + 1) * LANES])
        vd_ref[0, :, (2 * j) * LANES:(2 * j + 1) * LANES] = ve.astype(BF16)
        vd_ref[0, :, (2 * j + 1) * LANES:(2 * j + 2) * LANES] = vo.astype(BF16)


def _ctxkv(ctx, csh1, csc1, g1n, w_kv):
    bn, n, d = ctx.shape
    const2 = lambda b: (0, 0)
    blk = lambda b: (b, 0, 0)
    return pl.pallas_call(
        _ctxkv_kernel,
        grid=(bn,),
        in_specs=[pl.BlockSpec((1, n, d), blk),
                  pl.BlockSpec((1, d), const2),
                  pl.BlockSpec((1, d), const2),
                  pl.BlockSpec((1, d), const2),
                  pl.BlockSpec((d, 2 * KV_WIDTH), const2)],
        out_specs=[pl.BlockSpec((1, n, KV_DUP_WIDTH), blk), pl.BlockSpec((1, n, KV_DUP_WIDTH), blk)],
        out_shape=[jax.ShapeDtypeStruct((bn, n, KV_DUP_WIDTH), BF16)] * 2,
        compiler_params=_params("arbitrary"),
        name="ctxkv",
    )(ctx, csh1, csc1, g1n, w_kv)


def _attn_kernel(sink_ref, q_ref, kp_ref, km_ref, kn_ref, vp_ref, vm_ref, vn_ref, kx_ref, vx_ref, o_ref):
    rep = N_HEADS // N_KV_HEADS
    rows = rep * BLOCK
    n_loc = 3 * BLOCK
    r = lax.broadcasted_iota(jnp.int32, (BLOCK, n_loc), 0)
    s = lax.broadcasted_iota(jnp.int32, (BLOCK, n_loc), 1)
    in_band = jnp.abs(s - BLOCK - r) <= WINDOW
    lo = lax.broadcasted_iota(jnp.int32, (BLOCK, LANES), 1) < HEAD_DIM
    rcol = lax.broadcasted_iota(jnp.int32, (rows, 1), 0)
    zero = jnp.zeros((BLOCK, LANES), BF16)

    def key_blocks(p_ref, m_ref, n_ref, ls):
        mids = [m_ref[0, i * BLOCK:(i + 1) * BLOCK, ls] for i in range(ATTN_QBLOCKS)]
        return [p_ref[0, :, ls]] + mids + [n_ref[0, :, ls]]

    for sub in range(ATTN_QBLOCKS):
        n = pl.program_id(1) * ATTN_QBLOCKS + sub
        qrows = slice(sub * BLOCK, (sub + 1) * BLOCK)
        key_pos = (n - 1) * BLOCK + s
        bias = jnp.where(in_band & (key_pos >= 0) & (key_pos < SEQ), 0.0, MASK_VALUE)
        bias = jnp.concatenate([bias] * rep, axis=0)
        for kh in range(N_KV_HEADS):
            ls = slice(kh * LANES, (kh + 1) * LANES)
            parts = []
            for j in (2 * kh, 2 * kh + 1):
                qc = q_ref[0, qrows, j * LANES:(j + 1) * LANES]
                parts += [jnp.where(lo, qc, zero), jnp.where(lo, zero, qc)]
            lhs = jnp.concatenate(parts, axis=0)
            k_all = jnp.concatenate(key_blocks(kp_ref, km_ref, kn_ref, ls)[sub:sub + 3] + [kx_ref[0, :, ls]], axis=0)
            v_all = jnp.concatenate(key_blocks(vp_ref, vm_ref, vn_ref, ls)[sub:sub + 3] + [vx_ref[0, :, ls]], axis=0)
            logits = lax.dot_general(lhs, k_all, (((1,), (1,)), ((), ())), preferred_element_type=F32)
            logits = jnp.concatenate([logits[:, :n_loc] + bias, logits[:, n_loc:]], axis=1)
            sink = LOG2E * jnp.where(
                rcol < BLOCK, sink_ref[rep * kh],
                jnp.where(rcol < 2 * BLOCK, sink_ref[rep * kh + 1],
                          jnp.where(rcol < 3 * BLOCK, sink_ref[rep * kh + 2], sink_ref[rep * kh + 3])))
            m = jnp.maximum(jnp.max(logits, axis=1, keepdims=True), sink)
            p = jnp.exp2(logits - m).astype(BF16)
            o = jnp.dot(p, v_all, preferred_element_type=F32)
            od = o + jnp.exp2(sink - m)
            for c in range(2):
                g_lo = slice((2 * c) * BLOCK, (2 * c + 1) * BLOCK)
                g_hi = slice((2 * c + 1) * BLOCK, (2 * c + 2) * BLOCK)
                out_lo = o[g_lo] / pltpu.roll(od[g_lo], HEAD_DIM, axis=1)
                out_hi = pltpu.roll(o[g_hi], HEAD_DIM, axis=1) / od[g_hi]
                o_ref[0, qrows, (2 * kh + c) * LANES:(2 * kh + c + 1) * LANES] = jnp.where(
                    lo, out_lo, out_hi).astype(BF16)


def _attn(sink, q, kd, vd, kxd, vxd):
    bn, L, _ = q.shape
    nb = L // BLOCK
    qb = ATTN_QBLOCKS
    mid = lambda b, i: (b, i, 0)
    prev = lambda b, i: (b, jnp.maximum(i * qb - 1, 0), 0)
    nxt = lambda b, i: (b, jnp.minimum(i * qb + qb, nb - 1), 0)
    per_b = lambda b, i: (b, 0, 0)
    edge_blk = (1, BLOCK, KV_DUP_WIDTH)
    mid_blk = (1, qb * BLOCK, KV_DUP_WIDTH)
    return pl.pallas_call(
        _attn_kernel,
        grid=(bn, nb // qb),
        in_specs=[pl.BlockSpec(memory_space=pltpu.SMEM),
                  pl.BlockSpec((1, qb * BLOCK, B_WIDTH), mid),
                  pl.BlockSpec(edge_blk, prev), pl.BlockSpec(mid_blk, mid), pl.BlockSpec(edge_blk, nxt),
                  pl.BlockSpec(edge_blk, prev), pl.BlockSpec(mid_blk, mid), pl.BlockSpec(edge_blk, nxt),
                  pl.BlockSpec((1, CTX_LEN, KV_DUP_WIDTH), per_b),
                  pl.BlockSpec((1, CTX_LEN, KV_DUP_WIDTH), per_b)],
        out_specs=pl.BlockSpec((1, qb * BLOCK, B_WIDTH), mid),
        out_shape=jax.ShapeDtypeStruct((bn, L, B_WIDTH), BF16),
        compiler_params=_params("arbitrary", "arbitrary"),
        name="attn",
    )(sink, q, kd, kd, kd, vd, vd, vd, kxd, vxd)


def _merge_kernel(x_ref, ya_ref, yb_ref, ga_ref, gb_ref, wa_ref, wb_ref, wo_ref, g1_ref, sh_ref, sc_ref,
                  gn_ref, wr_ref, x1_ref, h2_ref, aff_ref):
    pa = jnp.dot(ya_ref[0], wa_ref[...], preferred_element_type=F32)
    pb = jnp.dot(yb_ref[0], wb_ref[...], preferred_element_type=F32)
    merged = (ga_ref[0].astype(F32) * pa + gb_ref[0].astype(F32) * pb).astype(BF16)
    x1 = x_ref[0] + g1_ref[0] * jnp.dot(merged, wo_ref[...], preferred_element_type=F32)
    x1_ref[0] = x1
    h2 = _rms_modulate(x1, gn_ref[...], sh_ref[0], sc_ref[0]).astype(BF16)
    h2_ref[0] = h2
    logits = jnp.dot(h2, wr_ref[...], preferred_element_type=F32)
    e = jnp.exp(logits - jnp.max(logits, axis=-1, keepdims=True))
    aff_ref[0] = e / jnp.sum(e, axis=-1, keepdims=True)


def _merge(x, ya, yb, ga, gb, wa, wb, wo, g1, sh2, sc2, g2n, wr):
    bn, L, d = x.shape
    tm = TOK_TILE
    tok = lambda b, i: (b, i, 0)
    per_b = lambda b, i: (b, 0, 0)
    const2 = lambda b, i: (0, 0)
    return pl.pallas_call(
        _merge_kernel,
        grid=(bn, L // tm),
        in_specs=[pl.BlockSpec((1, tm, d), tok)] * 5
        + [pl.BlockSpec((d, d), const2)] * 3
        + [pl.BlockSpec((1, 1, d), per_b)] * 3
        + [pl.BlockSpec((1, d), const2), pl.BlockSpec((d, N_EXPERTS), const2)],
        out_specs=[pl.BlockSpec((1, tm, d), tok), pl.BlockSpec((1, tm, d), tok),
                   pl.BlockSpec((1, tm, N_EXPERTS), tok)],
        out_shape=[jax.ShapeDtypeStruct((bn, L, d), F32), jax.ShapeDtypeStruct((bn, L, d), BF16),
                   jax.ShapeDtypeStruct((bn, L, N_EXPERTS), F32)],
        compiler_params=_params("arbitrary", "arbitrary"),
        name="merge",
    )(x, ya, yb, ga, gb, wa, wb, wo, g1, sh2, sc2, g2n, wr)


def _excl_prefix_count(m, upper):
    n = m.shape[1]
    outs = []
    carry = jnp.zeros((m.shape[0], 1), F32)
    for blk in range(n // CUMSUM_BLOCK):
        mb = m[:, blk * CUMSUM_BLOCK:(blk + 1) * CUMSUM_BLOCK]
        outs.append(jnp.dot(mb, upper, preferred_element_type=F32) + carry)
        carry = carry + jnp.sum(mb.astype(F32), axis=1, keepdims=True)
    return jnp.concatenate(outs, axis=1)


def _select_kernel(aff_ref, slot_ref, off_ref):
    x = aff_ref[0]
    one = jnp.ones(x.shape, F32)
    zero = jnp.zeros(x.shape, F32)

    def count(mask):
        return jnp.sum(jnp.where(mask, one, zero), axis=1, keepdims=True)

    def body(i, t):
        cand = t | jnp.left_shift(jnp.int32(1), 30 - i)
        return jnp.where(count(x >= lax.bitcast_convert_type(cand, F32)) >= CAPACITY, cand, t)

    thr = lax.bitcast_convert_type(
        lax.fori_loop(0, 31, body, jnp.zeros((x.shape[0], 1), jnp.int32)), F32)
    gt = x > thr
    eq = x == thr
    need = CAPACITY - count(gt)
    ri = lax.broadcasted_iota(jnp.int32, (CUMSUM_BLOCK, CUMSUM_BLOCK), 0)
    ci = lax.broadcasted_iota(jnp.int32, (CUMSUM_BLOCK, CUMSUM_BLOCK), 1)
    upper = jnp.where(ri < ci, 1.0, 0.0).astype(BF16)
    eq_rank = _excl_prefix_count(jnp.where(eq, one, zero).astype(BF16), upper)
    sel = gt | (eq & (eq_rank < need))
    slot = _excl_prefix_count(jnp.where(sel, one, zero).astype(BF16), upper)
    slot_ref[0] = jnp.where(sel, slot, -1.0).astype(jnp.int32)
    lane = lax.broadcasted_iota(jnp.int32, (x.shape[0], LANES), 1)
    off = jnp.zeros((x.shape[0], LANES), F32)
    for j in range(x.shape[1] // WIN_TOK_TILE):
        off = jnp.where(lane == j, slot[:, j * WIN_TOK_TILE:j * WIN_TOK_TILE + 1], off)
    off_ref[0] = off.astype(jnp.int32)


def _select(aff_t):
    bn, e, n = aff_t.shape
    rows = bn * e
    blk = lambda i: (0, 0, 0)
    slot, off = pl.pallas_call(
        _select_kernel,
        grid=(1,),
        in_specs=[pl.BlockSpec((1, rows, n), blk)],
        out_specs=[pl.BlockSpec((1, rows, n), blk), pl.BlockSpec((1, rows, LANES), blk)],
        out_shape=[jax.ShapeDtypeStruct((1, rows, n), jnp.int32),
                   jax.ShapeDtypeStruct((1, rows, LANES), jnp.int32)],
        compiler_params=_params("arbitrary"),
        name="select",
    )(aff_t.reshape(1, rows, n))
    return slot.reshape(bn, e, n), off.reshape(bn, e, LANES)


def _window_plan(off):
    n_tiles = SEQ // WIN_TOK_TILE
    first = off[:, :, :n_tiles]
    end = jnp.concatenate([first[:, :, 1:], jnp.full_like(first[:, :, :1], CAPACITY)], axis=2)
    start = jnp.minimum((first // BF16_SUBLANES) * BF16_SUBLANES, CAPACITY - WIN_SLOTS)
    overflow = jnp.any(end - start > WIN_SLOTS, axis=1).astype(jnp.int32)
    return jnp.swapaxes(start, 1, 2).reshape(-1), overflow.reshape(-1)


GATHER_GROUP = 8


def _gather_kernel(start_ref, ovf_ref, slot_ref, h_ref, xe_ref):
    b = pl.program_id(0)
    j = pl.program_id(1)
    n_tiles = pl.num_programs(1)
    tile = h_ref.shape[1]

    @pl.when(j == 0)
    def _():
        xe_ref[...] = jnp.zeros_like(xe_ref)

    h = h_ref[0]
    slot = slot_ref[0].astype(F32)

    def onehot(e, height, first):
        row = lax.broadcasted_iota(jnp.int32, (height, tile), 0).astype(F32)
        return jnp.where(row == slot[e:e + 1, :] - first, 1.0, 0.0).astype(BF16)

    def accumulate(e, rows, part):
        xe_ref[0, e, rows, :] = (xe_ref[0, e, rows, :].astype(F32) + part).astype(BF16)

    overflow = ovf_ref[b * n_tiles + j] != 0

    @pl.when(jnp.logical_not(overflow))
    def _():
        for e0 in range(0, N_EXPERTS, GATHER_GROUP):
            starts = [pl.multiple_of(start_ref[(b * n_tiles + j) * N_EXPERTS + e], BF16_SUBLANES)
                      for e in range(e0, e0 + GATHER_GROUP)]
            lhs = jnp.concatenate([onehot(e0 + i, WIN_SLOTS, s.astype(F32)) for i, s in enumerate(starts)],
                                  axis=0)
            part = jnp.dot(lhs, h, preferred_element_type=F32)
            for i, s in enumerate(starts):
                accumulate(e0 + i, pl.ds(s, WIN_SLOTS), part[i * WIN_SLOTS:(i + 1) * WIN_SLOTS])

    @pl.when(overflow)
    def _():
        for e in range(N_EXPERTS):
            accumulate(e, slice(None), jnp.dot(onehot(e, CAPACITY, 0.0), h, preferred_element_type=F32))


def _gather(start, overflow, slot, h2):
    bn, n, d = h2.shape
    tile = WIN_TOK_TILE
    return pl.pallas_call(
        _gather_kernel,
        grid_spec=pltpu.PrefetchScalarGridSpec(
            num_scalar_prefetch=2,
            grid=(bn, n // tile),
            in_specs=[pl.BlockSpec((1, N_EXPERTS, tile), lambda b, i, *_: (b, 0, i)),
                      pl.BlockSpec((1, tile, d), lambda b, i, *_: (b, i, 0))],
            out_specs=pl.BlockSpec((1, N_EXPERTS, CAPACITY, d), lambda b, i, *_: (b, 0, 0, 0))),
        out_shape=jax.ShapeDtypeStruct((bn, N_EXPERTS, CAPACITY, d), BF16),
        compiler_params=_params("arbitrary", "arbitrary"),
        name="gather",
    )(start, overflow, slot, h2)


def _ffn_kernel(xe_ref, wg_ref, wu_ref, wd_ref, y_ref, wg_s, wu_s, wd_s):
    r = pl.program_id(0)
    step = pl.program_id(1)
    n_experts = pl.num_programs(0) - 1
    n_chunks = wg_s.shape[1]

    @pl.when(r < n_experts)
    def _():
        slot = lax.rem(r, 2)
        wg_s[slot, step] = wg_ref[0].astype(BF16)
        wu_s[slot, step] = wu_ref[0].astype(BF16)
        wd_s[slot, step] = wd_ref[0].astype(BF16)

    @pl.when(r == 0)
    def _():
        y_ref[...] = jnp.zeros_like(y_ref)

    @pl.when(r > 0)
    def _():
        slot = lax.rem(r - 1, 2)
        xe = xe_ref[0, 0]
        acc = jnp.zeros((xe.shape[0], wd_s.shape[3]), F32)
        for j in range(n_chunks):
            g = jnp.dot(xe, wg_s[slot, j], preferred_element_type=F32)
            u = jnp.dot(xe, wu_s[slot, j], preferred_element_type=F32)
            a = ((g * jax.nn.sigmoid(g)) * u).astype(BF16)
            acc = acc + jnp.dot(a, wd_s[slot, j], preferred_element_type=F32)
        y_ref[0, 0] = acc.astype(BF16)


def _ffn(xe, wg, wu, wd):
    bn, ne, c, d = xe.shape
    f = wg.shape[2]
    fc = f // bn
    assert f % bn == 0 and fc % LANES == 0
    staged = lambda r: jnp.minimum(r, ne - 1)
    chunk = lambda r, s: jnp.where(r < ne, s, bn - 1)
    computed = lambda r: jnp.maximum(r - 1, 0)
    return pl.pallas_call(
        _ffn_kernel,
        grid=(ne + 1, bn),
        in_specs=[pl.BlockSpec((1, 1, c, d), lambda r, s: (s, computed(r), 0, 0)),
                  pl.BlockSpec((1, d, fc), lambda r, s: (staged(r), 0, chunk(r, s))),
                  pl.BlockSpec((1, d, fc), lambda r, s: (staged(r), 0, chunk(r, s))),
                  pl.BlockSpec((1, fc, d), lambda r, s: (staged(r), chunk(r, s), 0))],
        out_specs=pl.BlockSpec((1, 1, c, d), lambda r, s: (s, jnp.where(r == 0, ne, r - 1), 0, 0)),
        out_shape=jax.ShapeDtypeStruct((bn, ne + 1, c, d), BF16),
        scratch_shapes=[pltpu.VMEM((2, bn, d, fc), BF16), pltpu.VMEM((2, bn, d, fc), BF16),
                        pltpu.VMEM((2, bn, fc, d), BF16)],
        compiler_params=_params("arbitrary", "arbitrary"),
        name="ffn",
    )(xe, wg, wu, wd)


def _combine_kernel(start_ref, ovf_ref, slot_ref, aff_ref, y_ref, x1_ref, g2_ref, gf_ref, o_ref):
    b = pl.program_id(0)
    j = pl.program_id(1)
    n_tiles = pl.num_programs(1)
    tile = slot_ref.shape[1]
    slot = slot_ref[0].astype(F32)
    aff = aff_ref[0]

    def finish(moe):
        xo = x1_ref[0] + g2_ref[0] * moe
        o_ref[0] = (xo * lax.rsqrt(jnp.mean(xo * xo, axis=-1, keepdims=True) + EPS)) * gf_ref[...]

    def weighted_onehot(e, width, first):
        col = lax.broadcasted_iota(jnp.int32, (tile, width), 1).astype(F32)
        return jnp.where(col == slot[:, e:e + 1] - first, aff[:, e:e + 1], 0.0).astype(BF16)

    overflow = ovf_ref[b * n_tiles + j] != 0

    @pl.when(jnp.logical_not(overflow))
    def _():
        lhs, rhs = [], []
        for e in range(N_EXPERTS):
            start = pl.multiple_of(start_ref[(b * n_tiles + j) * N_EXPERTS + e], BF16_SUBLANES)
            lhs.append(weighted_onehot(e, WIN_SLOTS, start.astype(F32)))
            rhs.append(y_ref[0, e, pl.ds(start, WIN_SLOTS), :])
        finish(jnp.dot(jnp.concatenate(lhs, axis=1), jnp.concatenate(rhs, axis=0),
                       preferred_element_type=F32))

    @pl.when(overflow)
    def _():
        moe = jnp.zeros(x1_ref.shape[1:], F32)
        for e in range(N_EXPERTS):
            moe = moe + jnp.dot(weighted_onehot(e, CAPACITY, 0.0), y_ref[0, e], preferred_element_type=F32)
        finish(moe)


def _combine(start, overflow, slot_t, aff, y, x1, g2, gf):
    bn, n, d = x1.shape
    tile = WIN_TOK_TILE
    tok = lambda b, i, *_: (b, i, 0)
    return pl.pallas_call(
        _combine_kernel,
        grid_spec=pltpu.PrefetchScalarGridSpec(
            num_scalar_prefetch=2,
            grid=(bn, n // tile),
            in_specs=[pl.BlockSpec((1, tile, N_EXPERTS), tok),
                      pl.BlockSpec((1, tile, N_EXPERTS), tok),
                      pl.BlockSpec((1, N_EXPERTS, CAPACITY, d), lambda b, i, *_: (b, 0, 0, 0)),
                      pl.BlockSpec((1, tile, d), tok),
                      pl.BlockSpec((1, 1, d), lambda b, i, *_: (b, 0, 0)),
                      pl.BlockSpec((1, d), lambda b, i, *_: (0, 0))],
            out_specs=pl.BlockSpec((1, tile, d), tok)),
        out_shape=jax.ShapeDtypeStruct((bn, n, d), F32),
        compiler_params=_params("arbitrary", "arbitrary"),
        name="combine",
    )(start, overflow, slot_t, aff, y, x1, g2, gf)


def _rope_tables(L):
    pos = jnp.arange(L)
    row = (pos // GRID_W).astype(F32)
    col = (pos % GRID_W).astype(F32)
    half = HEAD_DIM // 2
    freqs = ROPE_BASE ** (-jnp.arange(0, half, 2, dtype=F32) / half)
    ang_r = row[:, None] * freqs[None, :]
    ang_c = col[:, None] * freqs[None, :]
    cos_h = jnp.concatenate([jnp.cos(ang_r), jnp.cos(ang_r), jnp.cos(ang_c), jnp.cos(ang_c)], axis=1)
    sin_h = jnp.concatenate([-jnp.sin(ang_r), jnp.sin(ang_r), -jnp.sin(ang_c), jnp.sin(ang_c)], axis=1)
    return jnp.tile(cos_h, (1, LANES // HEAD_DIM)), jnp.tile(sin_h, (1, LANES // HEAD_DIM))


def kernel(x, c, ctx, c_ctx, w_ada, b_ada, norm1_g, w_in, b_merge_gate, gmlp_ln_g, gmlp_ln_b, w_spatial,
           b_spatial, attn_sink, w_proj_a, w_proj_b, w_out, norm2_g, w_router, w_exp_gate, w_exp_up,
           w_exp_down, final_g):
    bn, L, d = x.shape
    depth = w_ada.shape[0]
    assert depth == 1, "the combine kernel fuses the final rmsnorm, so it must follow the only layer"
    cos_t, sin_t = _rope_tables(L)
    ada_rows = 16
    cc = jnp.zeros((ada_rows, d), F32).at[:bn].set(c).at[bn].set(c_ctx)

    for l in range(depth):
        mod = _ada(cc, w_ada[l], b_ada[l][None, :])
        sh1, sc1, g1, sh2, sc2, g2 = [mod[:bn, None, i * d:(i + 1) * d] for i in range(6)]
        csh1 = mod[bn:bn + 1, 0:d]
        csc1 = mod[bn:bn + 1, d:2 * d]
        w_in_b = w_in[l].astype(BF16)
        g1n = norm1_g[l][None, :]

        ya, q, kd, vd, ga, gb = _inproj(
            x, sh1, sc1, g1n, w_in_b, b_merge_gate[l][None, :], gmlp_ln_g[l][None, :],
            gmlp_ln_b[l][None, :], w_spatial[l].astype(BF16), b_spatial[l].T, cos_t, sin_t)
        kxd, vxd = _ctxkv(ctx, csh1, csc1, g1n, w_in_b[:, OFF_K:OFF_GATE])
        yb = _attn(attn_sink[l], q, kd, vd, kxd, vxd)

        x1, h2, aff = _merge(x, ya, yb, ga, gb, w_proj_a[l].astype(BF16), w_proj_b[l].astype(BF16),
                             w_out[l].astype(BF16), g1, sh2, sc2, norm2_g[l][None, :],
                             w_router[l].astype(BF16))
        slot, off = _select(jnp.swapaxes(aff, 1, 2))
        win_start, win_overflow = _window_plan(off)
        xe = _gather(win_start, win_overflow, slot, h2)
        y = _ffn(xe, w_exp_gate[l], w_exp_up[l], w_exp_down[l])
        x = _combine(win_start, win_overflow, jnp.swapaxes(slot, 1, 2), aff, y, x1, g2, final_g[None, :])
    return x
```

```python
import functools

import jax
import jax.numpy as jnp
from jax import lax
from jax.experimental import pallas as pl
from jax.experimental.pallas import tpu as pltpu

F32 = jnp.float32
BF16 = jnp.bfloat16

D_MODEL = 1024
SEQ = 4096
CTX_LEN = 256
GRID_W = 64
EPS = 1e-6
CHUNK = 128
A_GROUPS = 8
A_WIDTH = 1024
N_HEADS = 16
N_KV_HEADS = 4
HEAD_DIM = 64
B_WIDTH = N_HEADS * HEAD_DIM
KV_WIDTH = N_KV_HEADS * HEAD_DIM
WINDOW = 128
BLOCK = 128
ROPE_BASE = 10000.0
N_EXPERTS = 16
CAPACITY = 2 * SEQ // N_EXPERTS
D_EXPERT = 2048
OFF_U = 0
OFF_VA = OFF_U + A_WIDTH
OFF_Q = OFF_VA + A_WIDTH
OFF_K = OFF_Q + B_WIDTH
OFF_VB = OFF_K + KV_WIDTH
OFF_GATE = OFF_VB + KV_WIDTH
IN_COLS = OFF_GATE + 2 * D_MODEL

LANES = 128
KV_DUP_WIDTH = N_KV_HEADS * LANES
VMEM_LIMIT = 56 * 1024 * 1024
MASK_VALUE = -1e30
LOG2E = 1.4426950408889634

TOK_TILE = 512
ATTN_QBLOCKS = 8
WIN_TOK_TILE = 512
WIN_SLOTS = 128
BF16_SUBLANES = 16
CUMSUM_BLOCK = 512


def _params(*sem):
    return pltpu.CompilerParams(dimension_semantics=sem, vmem_limit_bytes=VMEM_LIMIT)


def _gelu_tanh(x):
    return 0.5 * x * (1.0 + jnp.tanh(0.7978845608028654 * (x + 0.044715 * (x * x * x))))


def _rms_modulate(x, gain, shift, scale):
    y = x * lax.rsqrt(jnp.mean(x * x, axis=-1, keepdims=True) + EPS)
    return (y * gain) * (1.0 + scale) + shift


def _dup_kv_heads(x):
    lane = lax.broadcasted_iota(jnp.int32, x.shape, 1)
    lo = lane < HEAD_DIM
    r = pltpu.roll(x, HEAD_DIM, axis=1)
    return jnp.where(lo, x, r), jnp.where(lo, r, x)


def _kv_heads_with_ones(x):
    lane = lax.broadcasted_iota(jnp.int32, x.shape, 1)
    lo = lane < HEAD_DIM
    return jnp.where(lo, x, 1.0), jnp.where(lo, pltpu.roll(x, HEAD_DIM, axis=1), 1.0)


def _ada_kernel(c_ref, w_ref, b_ref, o_ref):
    c = c_ref[...]
    s = c * jax.nn.sigmoid(c)
    o_ref[...] = jnp.dot(s, w_ref[...], preferred_element_type=F32,
                         precision=lax.Precision.HIGHEST) + b_ref[...]


def _ada(cc, w, b):
    rows, d = cc.shape
    cols = w.shape[1]
    tn = 1536
    return pl.pallas_call(
        _ada_kernel,
        grid=(cols // tn,),
        in_specs=[pl.BlockSpec((rows, d), lambda j: (0, 0)),
                  pl.BlockSpec((d, tn), lambda j: (0, j)),
                  pl.BlockSpec((1, tn), lambda j: (0, j))],
        out_specs=pl.BlockSpec((rows, tn), lambda j: (0, j)),
        out_shape=jax.ShapeDtypeStruct((rows, cols), F32),
        compiler_params=_params("arbitrary"),
        name="ada",
    )(cc, w, b)


def _inproj_kernel(x_ref, sh_ref, sc_ref, g_ref, w_ref, bg_ref, lng_ref, lnb_ref, ws_ref, bs_ref,
                   cos_ref, sin_ref, ya_ref, q_ref, kd_ref, vd_ref, ga_ref, gb_ref):
    tm = x_ref.shape[1]
    h = _rms_modulate(x_ref[0], g_ref[...], sh_ref[0], sc_ref[0]).astype(BF16)

    def proj(lo, hi):
        return jnp.dot(h, w_ref[:, lo:hi], preferred_element_type=F32)

    u = _gelu_tanh(proj(OFF_U, OFF_VA))
    v = _gelu_tanh(proj(OFF_VA, OFF_Q))
    mu = jnp.mean(v, axis=-1, keepdims=True)
    vc = v - mu
    var = jnp.mean(vc * vc, axis=-1, keepdims=True)
    vln = ((vc * lax.rsqrt(var + EPS)) * lng_ref[...] + lnb_ref[...]).astype(BF16)
    nc = tm // CHUNK
    gd = A_WIDTH // A_GROUPS
    for g in range(A_GROUPS):
        rhs = jnp.concatenate(
            [vln[c * CHUNK:(c + 1) * CHUNK, g * gd:(g + 1) * gd] for c in range(nc)], axis=1)
        mixed = jnp.dot(ws_ref[g], rhs, preferred_element_type=F32) + bs_ref[:, g:g + 1]
        for c in range(nc):
            ya_ref[0, c * CHUNK:(c + 1) * CHUNK, g * gd:(g + 1) * gd] = (
                u[c * CHUNK:(c + 1) * CHUNK, g * gd:(g + 1) * gd]
                * mixed[:, c * CHUNK:(c + 1) * CHUNK]).astype(BF16)

    cos = cos_ref[...]
    sin = sin_ref[...]
    lane = lax.broadcasted_iota(jnp.int32, (tm, LANES), 1)
    first_half = (lane % 32) < 16

    def rope(t):
        partner = jnp.where(first_half, pltpu.roll(t, LANES - 16, axis=1), pltpu.roll(t, 16, axis=1))
        return t * cos + partner * sin

    q = proj(OFF_Q, OFF_K)
    for j in range(B_WIDTH // LANES):
        q_ref[0, :, j * LANES:(j + 1) * LANES] = (
            rope(q[:, j * LANES:(j + 1) * LANES]) * (HEAD_DIM ** -0.5 * LOG2E)).astype(BF16)
    k = proj(OFF_K, OFF_VB)
    vb = proj(OFF_VB, OFF_GATE)
    for j in range(KV_WIDTH // LANES):
        ke, ko = _dup_kv_heads(rope(k[:, j * LANES:(j + 1) * LANES]))
        kd_ref[0, :, (2 * j) * LANES:(2 * j + 1) * LANES] = ke.astype(BF16)
        kd_ref[0, :, (2 * j + 1) * LANES:(2 * j + 2) * LANES] = ko.astype(BF16)
        ve, vo = _kv_heads_with_ones(vb[:, j * LANES:(j + 1) * LANES])
        vd_ref[0, :, (2 * j) * LANES:(2 * j + 1) * LANES] = ve.astype(BF16)
        vd_ref[0, :, (2 * j + 1) * LANES:(2 * j + 2) * LANES] = vo.astype(BF16)

    ga_ref[0] = jax.nn.sigmoid(proj(OFF_GATE, OFF_GATE + D_MODEL) + bg_ref[:, :D_MODEL]).astype(BF16)
    gb_ref[0] = jax.nn.sigmoid(proj(OFF_GATE + D_MODEL, IN_COLS) + bg_ref[:, D_MODEL:]).astype(BF16)


def _inproj(x, sh1, sc1, g1n, w_in, b_gate, ln_g, ln_b, w_s, b_s_t, cos_t, sin_t):
    bn, L, d = x.shape
    tm = TOK_TILE
    tok = lambda b, i: (b, i, 0)
    per_b = lambda b, i: (b, 0, 0)
    const2 = lambda b, i: (0, 0)
    out = lambda w: jax.ShapeDtypeStruct((bn, L, w), BF16)
    return pl.pallas_call(
        _inproj_kernel,
        grid=(bn, L // tm),
        in_specs=[pl.BlockSpec((1, tm, d), tok),
                  pl.BlockSpec((1, 1, d), per_b),
                  pl.BlockSpec((1, 1, d), per_b),
                  pl.BlockSpec((1, d), const2),
                  pl.BlockSpec((d, IN_COLS), const2),
                  pl.BlockSpec((1, 2 * d), const2),
                  pl.BlockSpec((1, A_WIDTH), const2),
                  pl.BlockSpec((1, A_WIDTH), const2),
                  pl.BlockSpec((A_GROUPS, CHUNK, CHUNK), lambda b, i: (0, 0, 0)),
                  pl.BlockSpec((CHUNK, A_GROUPS), const2),
                  pl.BlockSpec((tm, LANES), lambda b, i: (i, 0)),
                  pl.BlockSpec((tm, LANES), lambda b, i: (i, 0))],
        out_specs=[pl.BlockSpec((1, tm, A_WIDTH), tok),
                   pl.BlockSpec((1, tm, B_WIDTH), tok),
                   pl.BlockSpec((1, tm, KV_DUP_WIDTH), tok),
                   pl.BlockSpec((1, tm, KV_DUP_WIDTH), tok),
                   pl.BlockSpec((1, tm, d), tok),
                   pl.BlockSpec((1, tm, d), tok)],
        out_shape=[out(A_WIDTH), out(B_WIDTH), out(KV_DUP_WIDTH), out(KV_DUP_WIDTH), out(d), out(d)],
        compiler_params=_params("arbitrary", "arbitrary"),
        name="inproj",
    )(x, sh1, sc1, g1n, w_in, b_gate, ln_g, ln_b, w_s, b_s_t, cos_t, sin_t)


def _ctxkv_kernel(x_ref, sh_ref, sc_ref, g_ref, w_ref, kd_ref, vd_ref):
    h = _rms_modulate(x_ref[0], g_ref[...], sh_ref[...], sc_ref[...]).astype(BF16)
    kv = jnp.dot(h, w_ref[...], preferred_element_type=F32)
    for j in range(KV_WIDTH // LANES):
        ke, ko = _dup_kv_heads(kv[:, j * LANES:(j + 1) * LANES])
        kd_ref[0, :, (2 * j) * LANES:(2 * j + 1) * LANES] = ke.astype(BF16)
        kd_ref[0, :, (2 * j + 1) * LANES:(2 * j + 2) * LANES] = ko.astype(BF16)
        ve, vo = _kv_heads_with_ones(kv[:, KV_WIDTH + j * LANES:KV_WIDTH + (j + 1) * LANES])
        vd_ref[0, :, (2 * j) * LANES:(2 * j + 1) * LANES] = ve.astype(BF16)
        vd_ref[0, :, (2 * j + 1) * LANES:(2 * j + 2) * LANES] = vo.astype(BF16)


def _ctxkv(ctx, csh1, csc1, g1n, w_kv):
    bn, n, d = ctx.shape
    const2 = lambda b: (0, 0)
    blk = lambda b: (b, 0, 0)
    return pl.pallas_call(
        _ctxkv_kernel,
        grid=(bn,),
        in_specs=[pl.BlockSpec((1, n, d), blk),
                  pl.BlockSpec((1, d), const2),
                  pl.BlockSpec((1, d), const2),
                  pl.BlockSpec((1, d), const2),
                  pl.BlockSpec((d, 2 * KV_WIDTH), const2)],
        out_specs=[pl.BlockSpec((1, n, KV_DUP_WIDTH), blk), pl.BlockSpec((1, n, KV_DUP_WIDTH), blk)],
        out_shape=[jax.ShapeDtypeStruct((bn, n, KV_DUP_WIDTH), BF16)] * 2,
        compiler_params=_params("arbitrary"),
        name="ctxkv",
    )(ctx, csh1, csc1, g1n, w_kv)


def _attn_kernel(sink_ref, q_ref, kp_ref, km_ref, kn_ref, vp_ref, vm_ref, vn_ref, kx_ref, vx_ref, o_ref):
    rep = N_HEADS // N_KV_HEADS
    rows = rep * BLOCK
    n_loc = 3 * BLOCK
    r = lax.broadcasted_iota(jnp.int32, (BLOCK, n_loc), 0)
    s = lax.broadcasted_iota(jnp.int32, (BLOCK, n_loc), 1)
    in_band = jnp.abs(s - BLOCK - r) <= WINDOW
    lo = lax.broadcasted_iota(jnp.int32, (BLOCK, LANES), 1) < HEAD_DIM
    rcol = lax.broadcasted_iota(jnp.int32, (rows, 1), 0)
    zero = jnp.zeros((BLOCK, LANES), BF16)

    def key_blocks(p_ref, m_ref, n_ref, ls):
        mids = [m_ref[0, i * BLOCK:(i + 1) * BLOCK, ls] for i in range(ATTN_QBLOCKS)]
        return [p_ref[0, :, ls]] + mids + [n_ref[0, :, ls]]

    for sub in range(ATTN_QBLOCKS):
        n = pl.program_id(1) * ATTN_QBLOCKS + sub
        qrows = slice(sub * BLOCK, (sub + 1) * BLOCK)
        key_pos = (n - 1) * BLOCK + s
        bias = jnp.where(in_band & (key_pos >= 0) & (key_pos < SEQ), 0.0, MASK_VALUE)
        bias = jnp.concatenate([bias] * rep, axis=0)
        for kh in range(N_KV_HEADS):
            ls = slice(kh * LANES, (kh + 1) * LANES)
            parts = []
            for j in (2 * kh, 2 * kh + 1):
                qc = q_ref[0, qrows, j * LANES:(j + 1) * LANES]
                parts += [jnp.where(lo, qc, zero), jnp.where(lo, zero, qc)]
            lhs = jnp.concatenate(parts, axis=0)
            k_all = jnp.concatenate(key_blocks(kp_ref, km_ref, kn_ref, ls)[sub:sub + 3] + [kx_ref[0, :, ls]], axis=0)
            v_all = jnp.concatenate(key_blocks(vp_ref, vm_ref, vn_ref, ls)[sub:sub + 3] + [vx_ref[0, :, ls]], axis=0)
            logits = lax.dot_general(lhs, k_all, (((1,), (1,)), ((), ())), preferred_element_type=F32)
            logits = jnp.concatenate([logits[:, :n_loc] + bias, logits[:, n_loc:]], axis=1)
            sink = LOG2E * jnp.where(
                rcol < BLOCK, sink_ref[rep * kh],
                jnp.where(rcol < 2 * BLOCK, sink_ref[rep * kh + 1],
                          jnp.where(rcol < 3 * BLOCK, sink_ref[rep * kh + 2], sink_ref[rep * kh + 3])))
            m = jnp.maximum(jnp.max(logits, axis=1, keepdims=True), sink)
            p = jnp.exp2(logits - m).astype(BF16)
            o = jnp.dot(p, v_all, preferred_element_type=F32)
            od = o + jnp.exp2(sink - m)
            for c in range(2):
                g_lo = slice((2 * c) * BLOCK, (2 * c + 1) * BLOCK)
                g_hi = slice((2 * c + 1) * BLOCK, (2 * c + 2) * BLOCK)
                out_lo = o[g_lo] / pltpu.roll(od[g_lo], HEAD_DIM, axis=1)
                out_hi = pltpu.roll(o[g_hi], HEAD_DIM, axis=1) / od[g_hi]
                o_ref[0, qrows, (2 * kh + c) * LANES:(2 * kh + c + 1) * LANES] = jnp.where(
                    lo, out_lo, out_hi).astype(BF16)


def _attn(sink, q, kd, vd, kxd, vxd):
    bn, L, _ = q.shape
    nb = L // BLOCK
    qb = ATTN_QBLOCKS
    mid = lambda b, i: (b, i, 0)
    prev = lambda b, i: (b, jnp.maximum(i * qb - 1, 0), 0)
    nxt = lambda b, i: (b, jnp.minimum(i * qb + qb, nb - 1), 0)
    per_b = lambda b, i: (b, 0, 0)
    edge_blk = (1, BLOCK, KV_DUP_WIDTH)
    mid_blk = (1, qb * BLOCK, KV_DUP_WIDTH)
    return pl.pallas_call(
        _attn_kernel,
        grid=(bn, nb // qb),
        in_specs=[pl.BlockSpec(memory_space=pltpu.SMEM),
                  pl.BlockSpec((1, qb * BLOCK, B_WIDTH), mid),
                  pl.BlockSpec(edge_blk, prev), pl.BlockSpec(mid_blk, mid), pl.BlockSpec(edge_blk, nxt),
                  pl.BlockSpec(edge_blk, prev), pl.BlockSpec(mid_blk, mid), pl.BlockSpec(edge_blk, nxt),
                  pl.BlockSpec((1, CTX_LEN, KV_DUP_WIDTH), per_b),
                  pl.BlockSpec((1, CTX_LEN, KV_DUP_WIDTH), per_b)],
        out_specs=pl.BlockSpec((1, qb * BLOCK, B_WIDTH), mid),
        out_shape=jax.ShapeDtypeStruct((bn, L, B_WIDTH), BF16),
        compiler_params=_params("arbitrary", "arbitrary"),
        name="attn",
    )(sink, q, kd, kd, kd, vd, vd, vd, kxd, vxd)


def _merge_kernel(x_ref, ya_ref, yb_ref, ga_ref, gb_ref, wa_ref, wb_ref, wo_ref, g1_ref, sh_ref, sc_ref,
                  gn_ref, wr_ref, x1_ref, h2_ref, aff_ref):
    pa = jnp.dot(ya_ref[0], wa_ref[...], preferred_element_type=F32)
    pb = jnp.dot(yb_ref[0], wb_ref[...], preferred_element_type=F32)
    merged = (ga_ref[0].astype(F32) * pa + gb_ref[0].astype(F32) * pb).astype(BF16)
    x1 = x_ref[0] + g1_ref[0] * jnp.dot(merged, wo_ref[...], preferred_element_type=F32)
    x1_ref[0] = x1
    h2 = _rms_modulate(x1, gn_ref[...], sh_ref[0], sc_ref[0]).astype(BF16)
    h2_ref[0] = h2
    logits = jnp.dot(h2, wr_ref[...], preferred_element_type=F32)
    e = jnp.exp(logits - jnp.max(logits, axis=-1, keepdims=True))
    aff_ref[0] = e / jnp.sum(e, axis=-1, keepdims=True)


def _merge(x, ya, yb, ga, gb, wa, wb, wo, g1, sh2, sc2, g2n, wr):
    bn, L, d = x.shape
    tm = TOK_TILE
    tok = lambda b, i: (b, i, 0)
    per_b = lambda b, i: (b, 0, 0)
    const2 = lambda b, i: (0, 0)
    return pl.pallas_call(
        _merge_kernel,
        grid=(bn, L // tm),
        in_specs=[pl.BlockSpec((1, tm, d), tok)] * 5
        + [pl.BlockSpec((d, d), const2)] * 3
        + [pl.BlockSpec((1, 1, d), per_b)] * 3
        + [pl.BlockSpec((1, d), const2), pl.BlockSpec((d, N_EXPERTS), const2)],
        out_specs=[pl.BlockSpec((1, tm, d), tok), pl.BlockSpec((1, tm, d), tok),
                   pl.BlockSpec((1, tm, N_EXPERTS), tok)],
        out_shape=[jax.ShapeDtypeStruct((bn, L, d), F32), jax.ShapeDtypeStruct((bn, L, d), BF16),
                   jax.ShapeDtypeStruct((bn, L, N_EXPERTS), F32)],
        compiler_params=_params("arbitrary", "arbitrary"),
        name="merge",
    )(x, ya, yb, ga, gb, wa, wb, wo, g1, sh2, sc2, g2n, wr)


def _excl_prefix_count(m, upper):
    n = m.shape[1]
    outs = []
    carry = jnp.zeros((m.shape[0], 1), F32)
    for blk in range(n // CUMSUM_BLOCK):
        mb = m[:, blk * CUMSUM_BLOCK:(blk + 1) * CUMSUM_BLOCK]
        outs.append(jnp.dot(mb, upper, preferred_element_type=F32) + carry)
        carry = carry + jnp.sum(mb.astype(F32), axis=1, keepdims=True)
    return jnp.concatenate(outs, axis=1)


def _select_kernel(aff_ref, slot_ref, off_ref):
    x = aff_ref[0]
    one = jnp.ones(x.shape, F32)
    zero = jnp.zeros(x.shape, F32)

    def count(mask):
        return jnp.sum(jnp.where(mask, one, zero), axis=1, keepdims=True)

    def body(i, t):
        cand = t | jnp.left_shift(jnp.int32(1), 30 - i)
        return jnp.where(count(x >= lax.bitcast_convert_type(cand, F32)) >= CAPACITY, cand, t)

    thr = lax.bitcast_convert_type(
        lax.fori_loop(0, 31, body, jnp.zeros((x.shape[0], 1), jnp.int32)), F32)
    gt = x > thr
    eq = x == thr
    need = CAPACITY - count(gt)
    ri = lax.broadcasted_iota(jnp.int32, (CUMSUM_BLOCK, CUMSUM_BLOCK), 0)
    ci = lax.broadcasted_iota(jnp.int32, (CUMSUM_BLOCK, CUMSUM_BLOCK), 1)
    upper = jnp.where(ri < ci, 1.0, 0.0).astype(BF16)
    eq_rank = _excl_prefix_count(jnp.where(eq, one, zero).astype(BF16), upper)
    sel = gt | (eq & (eq_rank < need))
    slot = _excl_prefix_count(jnp.where(sel, one, zero).astype(BF16), upper)
    slot_ref[0] = jnp.where(sel, slot, -1.0).astype(jnp.int32)
    lane = lax.broadcasted_iota(jnp.int32, (x.shape[0], LANES), 1)
    off = jnp.zeros((x.shape[0], LANES), F32)
    for j in range(x.shape[1] // WIN_TOK_TILE):
        off = jnp.where(lane == j, slot[:, j * WIN_TOK_TILE:j * WIN_TOK_TILE + 1], off)
    off_ref[0] = off.astype(jnp.int32)


def _select(aff_t):
    bn, e, n = aff_t.shape
    rows = bn * e
    blk = lambda i: (0, 0, 0)
    slot, off = pl.pallas_call(
        _select_kernel,
        grid=(1,),
        in_specs=[pl.BlockSpec((1, rows, n), blk)],
        out_specs=[pl.BlockSpec((1, rows, n), blk), pl.BlockSpec((1, rows, LANES), blk)],
        out_shape=[jax.ShapeDtypeStruct((1, rows, n), jnp.int32),
                   jax.ShapeDtypeStruct((1, rows, LANES), jnp.int32)],
        compiler_params=_params("arbitrary"),
        name="select",
    )(aff_t.reshape(1, rows, n))
    return slot.reshape(bn, e, n), off.reshape(bn, e, LANES)


def _window_plan(off):
    n_tiles = SEQ // WIN_TOK_TILE
    first = off[:, :, :n_tiles]
    end = jnp.concatenate([first[:, :, 1:], jnp.full_like(first[:, :, :1], CAPACITY)], axis=2)
    start = jnp.minimum((first // BF16_SUBLANES) * BF16_SUBLANES, CAPACITY - WIN_SLOTS)
    overflow = jnp.any(end - start > WIN_SLOTS, axis=1).astype(jnp.int32)
    return jnp.swapaxes(start, 1, 2).reshape(-1), overflow.reshape(-1)


GATHER_GROUP = 8
FFN_SAMPLES = 2
FFN_CHUNK = 256


def _gather_kernel(start_ref, ovf_ref, slot_ref, h_ref, xe_ref):
    b = pl.program_id(0)
    j = pl.program_id(1)
    n_tiles = pl.num_programs(1)
    tile = h_ref.shape[1]

    @pl.when(j == 0)
    def _():
        xe_ref[...] = jnp.zeros_like(xe_ref)

    h = h_ref[0]
    slot = slot_ref[0].astype(F32)

    def onehot(e, height, first):
        row = lax.broadcasted_iota(jnp.int32, (height, tile), 0).astype(F32)
        return jnp.where(row == slot[e:e + 1, :] - first, 1.0, 0.0).astype(BF16)

    def accumulate(e, rows, part):
        xe_ref[0, e, rows, :] = (xe_ref[0, e, rows, :].astype(F32) + part).astype(BF16)

    overflow = ovf_ref[b * n_tiles + j] != 0

    @pl.when(jnp.logical_not(overflow))
    def _():
        for e0 in range(0, N_EXPERTS, GATHER_GROUP):
            starts = [pl.multiple_of(start_ref[(b * n_tiles + j) * N_EXPERTS + e], BF16_SUBLANES)
                      for e in range(e0, e0 + GATHER_GROUP)]
            lhs = jnp.concatenate([onehot(e0 + i, WIN_SLOTS, s.astype(F32)) for i, s in enumerate(starts)],
                                  axis=0)
            part = jnp.dot(lhs, h, preferred_element_type=F32)
            for i, s in enumerate(starts):
                accumulate(e0 + i, pl.ds(s, WIN_SLOTS), part[i * WIN_SLOTS:(i + 1) * WIN_SLOTS])

    @pl.when(overflow)
    def _():
        for e in range(N_EXPERTS):
            accumulate(e, slice(None), jnp.dot(onehot(e, CAPACITY, 0.0), h, preferred_element_type=F32))


def _gather(start, overflow, slot, h2):
    bn, n, d = h2.shape
    tile = WIN_TOK_TILE
    return pl.pallas_call(
        _gather_kernel,
        grid_spec=pltpu.PrefetchScalarGridSpec(
            num_scalar_prefetch=2,
            grid=(bn, n // tile),
            in_specs=[pl.BlockSpec((1, N_EXPERTS, tile), lambda b, i, *_: (b, 0, i)),
                      pl.BlockSpec((1, tile, d), lambda b, i, *_: (b, i, 0))],
            out_specs=pl.BlockSpec((1, N_EXPERTS, CAPACITY, d), lambda b, i, *_: (b, 0, 0, 0))),
        out_shape=jax.ShapeDtypeStruct((bn, N_EXPERTS, CAPACITY, d), BF16),
        compiler_params=_params("arbitrary", "arbitrary"),
        name="gather",
    )(start, overflow, slot, h2)


def _ffn_kernel(xe_ref, wg_ref, wu_ref, wd_ref, y_ref, wg_s, wu_s, wd_s):
    r = pl.program_id(0)
    step = pl.program_id(1)
    n_experts = pl.num_programs(0) - 1
    n_chunks, _, fc = wg_s.shape[1:]
    per_step = wg_ref.shape[2] // fc

    @pl.when(r < n_experts)
    def _():
        slot = lax.rem(r, 2)
        for i in range(per_step):
            wg_s[slot, step * per_step + i] = wg_ref[0, :, i * fc:(i + 1) * fc].astype(BF16)
            wu_s[slot, step * per_step + i] = wu_ref[0, :, i * fc:(i + 1) * fc].astype(BF16)
            wd_s[slot, step * per_step + i] = wd_ref[0, i * fc:(i + 1) * fc, :].astype(BF16)

    @pl.when(r == 0)
    def _():
        y_ref[...] = jnp.zeros_like(y_ref)

    @pl.when(r > 0)
    def _():
        slot = lax.rem(r - 1, 2)
        ns, _, c, d = xe_ref.shape
        xe = xe_ref[:, 0].reshape(ns * c, d)
        acc = jnp.zeros((ns * c, d), F32)
        for j in range(n_chunks):
            g = jnp.dot(xe, wg_s[slot, j], preferred_element_type=F32)
            u = jnp.dot(xe, wu_s[slot, j], preferred_element_type=F32)
            a = ((g * jax.nn.sigmoid(g)) * u).astype(BF16)
            acc = acc + jnp.dot(a, wd_s[slot, j], preferred_element_type=F32)
        y_ref[:, 0] = acc.astype(BF16).reshape(ns, c, d)


def _ffn(xe, wg, wu, wd):
    bn, ne, c, d = xe.shape
    f = wg.shape[2]
    ns = FFN_SAMPLES
    steps = bn // ns
    fc = FFN_CHUNK
    wb = f // steps
    assert bn % ns == 0 and f % steps == 0 and wb % fc == 0
    staged = lambda r: jnp.minimum(r, ne - 1)
    chunk = lambda r, s: jnp.where(r < ne, s, steps - 1)
    computed = lambda r: jnp.maximum(r - 1, 0)
    return pl.pallas_call(
        _ffn_kernel,
        grid=(ne + 1, steps),
        in_specs=[pl.BlockSpec((ns, 1, c, d), lambda r, s: (s, computed(r), 0, 0)),
                  pl.BlockSpec((1, d, wb), lambda r, s: (staged(r), 0, chunk(r, s))),
                  pl.BlockSpec((1, d, wb), lambda r, s: (staged(r), 0, chunk(r, s))),
                  pl.BlockSpec((1, wb, d), lambda r, s: (staged(r), chunk(r, s), 0))],
        out_specs=pl.BlockSpec((ns, 1, c, d), lambda r, s: (s, jnp.where(r == 0, ne, r - 1), 0, 0)),
        out_shape=jax.ShapeDtypeStruct((bn, ne + 1, c, d), BF16),
        scratch_shapes=[pltpu.VMEM((2, f // fc, d, fc), BF16), pltpu.VMEM((2, f // fc, d, fc), BF16),
                        pltpu.VMEM((2, f // fc, fc, d), BF16)],
        compiler_params=_params("arbitrary", "arbitrary"),
        name="ffn",
    )(xe, wg, wu, wd)


def _combine_kernel(start_ref, ovf_ref, slot_ref, aff_ref, y_ref, x1_ref, g2_ref, gf_ref, o_ref):
    b = pl.program_id(0)
    j = pl.program_id(1)
    n_tiles = pl.num_programs(1)
    tile = slot_ref.shape[1]
    slot = slot_ref[0].astype(F32)
    aff = aff_ref[0]

    def finish(moe):
        xo = x1_ref[0] + g2_ref[0] * moe
        o_ref[0] = (xo * lax.rsqrt(jnp.mean(xo * xo, axis=-1, keepdims=True) + EPS)) * gf_ref[...]

    def weighted_onehot(e, width, first):
        col = lax.broadcasted_iota(jnp.int32, (tile, width), 1).astype(F32)
        return jnp.where(col == slot[:, e:e + 1] - first, aff[:, e:e + 1], 0.0).astype(BF16)

    overflow = ovf_ref[b * n_tiles + j] != 0

    @pl.when(jnp.logical_not(overflow))
    def _():
        lhs, rhs = [], []
        for e in range(N_EXPERTS):
            start = pl.multiple_of(start_ref[(b * n_tiles + j) * N_EXPERTS + e], BF16_SUBLANES)
            lhs.append(weighted_onehot(e, WIN_SLOTS, start.astype(F32)))
            rhs.append(y_ref[0, e, pl.ds(start, WIN_SLOTS), :])
        finish(jnp.dot(jnp.concatenate(lhs, axis=1), jnp.concatenate(rhs, axis=0),
                       preferred_element_type=F32))

    @pl.when(overflow)
    def _():
        moe = jnp.zeros(x1_ref.shape[1:], F32)
        for e in range(N_EXPERTS):
            moe = moe + jnp.dot(weighted_onehot(e, CAPACITY, 0.0), y_ref[0, e], preferred_element_type=F32)
        finish(moe)


def _combine(start, overflow, slot_t, aff, y, x1, g2, gf):
    bn, n, d = x1.shape
    tile = WIN_TOK_TILE
    tok = lambda b, i, *_: (b, i, 0)
    return pl.pallas_call(
        _combine_kernel,
        grid_spec=pltpu.PrefetchScalarGridSpec(
            num_scalar_prefetch=2,
            grid=(bn, n // tile),
            in_specs=[pl.BlockSpec((1, tile, N_EXPERTS), tok),
                      pl.BlockSpec((1, tile, N_EXPERTS), tok),
                      pl.BlockSpec((1, N_EXPERTS, CAPACITY, d), lambda b, i, *_: (b, 0, 0, 0)),
                      pl.BlockSpec((1, tile, d), tok),
                      pl.BlockSpec((1, 1, d), lambda b, i, *_: (b, 0, 0)),
                      pl.BlockSpec((1, d), lambda b, i, *_: (0, 0))],
            out_specs=pl.BlockSpec((1, tile, d), tok)),
        out_shape=jax.ShapeDtypeStruct((bn, n, d), F32),
        compiler_params=_params("arbitrary", "arbitrary"),
        name="combine",
    )(start, overflow, slot_t, aff, y, x1, g2, gf)


def _rope_tables(L):
    pos = jnp.arange(L)
    row = (pos // GRID_W).astype(F32)
    col = (pos % GRID_W).astype(F32)
    half = HEAD_DIM // 2
    freqs = ROPE_BASE ** (-jnp.arange(0, half, 2, dtype=F32) / half)
    ang_r = row[:, None] * freqs[None, :]
    ang_c = col[:, None] * freqs[None, :]
    cos_h = jnp.concatenate([jnp.cos(ang_r), jnp.cos(ang_r), jnp.cos(ang_c), jnp.cos(ang_c)], axis=1)
    sin_h = jnp.concatenate([-jnp.sin(ang_r), jnp.sin(ang_r), -jnp.sin(ang_c), jnp.sin(ang_c)], axis=1)
    return jnp.tile(cos_h, (1, LANES // HEAD_DIM)), jnp.tile(sin_h, (1, LANES // HEAD_DIM))


def kernel(x, c, ctx, c_ctx, w_ada, b_ada, norm1_g, w_in, b_merge_gate, gmlp_ln_g, gmlp_ln_b, w_spatial,
           b_spatial, attn_sink, w_proj_a, w_proj_b, w_out, norm2_g, w_router, w_exp_gate, w_exp_up,
           w_exp_down, final_g):
    bn, L, d = x.shape
    depth = w_ada.shape[0]
    assert depth == 1, "the combine kernel fuses the final rmsnorm, so it must follow the only layer"
    cos_t, sin_t = _rope_tables(L)
    ada_rows = 16
    cc = jnp.zeros((ada_rows, d), F32).at[:bn].set(c).at[bn].set(c_ctx)

    for l in range(depth):
        mod = _ada(cc, w_ada[l], b_ada[l][None, :])
        sh1, sc1, g1, sh2, sc2, g2 = [mod[:bn, None, i * d:(i + 1) * d] for i in range(6)]
        csh1 = mod[bn:bn + 1, 0:d]
        csc1 = mod[bn:bn + 1, d:2 * d]
        w_in_b = w_in[l].astype(BF16)
        g1n = norm1_g[l][None, :]

        ya, q, kd, vd, ga, gb = _inproj(
            x, sh1, sc1, g1n, w_in_b, b_merge_gate[l][None, :], gmlp_ln_g[l][None, :],
            gmlp_ln_b[l][None, :], w_spatial[l].astype(BF16), b_spatial[l].T, cos_t, sin_t)
        kxd, vxd = _ctxkv(ctx, csh1, csc1, g1n, w_in_b[:, OFF_K:OFF_GATE])
        yb = _attn(attn_sink[l], q, kd, vd, kxd, vxd)

        x1, h2, aff = _merge(x, ya, yb, ga, gb, w_proj_a[l].astype(BF16), w_proj_b[l].astype(BF16),
                             w_out[l].astype(BF16), g1, sh2, sc2, norm2_g[l][None, :],
                             w_router[l].astype(BF16))
        slot, off = _select(jnp.swapaxes(aff, 1, 2))
        win_start, win_overflow = _window_plan(off)
        xe = _gather(win_start, win_overflow, slot, h2)
        y = _ffn(xe, w_exp_gate[l], w_exp_up[l], w_exp_down[l])
        x = _combine(win_start, win_overflow, jnp.swapaxes(slot, 1, 2), aff, y, x1, g2, final_g[None, :])
    return x
```
